```python
import math
import jax
import jax.numpy as jnp
from jax import lax
import numpy as np

D_MODEL = 1024
BATCH = 2
SEQ = 8192
DEPTH = 1
DEC_BATCH = 32
DEC_SEQ = 4
PAST_LEN = 8192
PAGE_SIZE = 128

HEAD_DIM = 64
H_NSA = 8
KV_NSA = 2
GRP = H_NSA // KV_NSA
H_FOX = 8
D_NSA = H_NSA * HEAD_DIM
D_FOX = H_FOX * HEAD_DIM
CMP_STRIDE = 16
CMP_LEN = 2 * CMP_STRIDE
CMP_HIDDEN = 128
SEL_BLOCK = 64
SUB_PER_SEL = SEL_BLOCK // CMP_STRIDE
TOP_N = 16
WINDOW = 512
Q_BLOCK = 128
T5_BUCKETS = 32
T5_EXACT = T5_BUCKETS // 2
T5_MAX_DIST = 1024
PLE_DIM = 256
FGATE_BIAS_INIT = 3.0
RMS_EPS = 1e-6
NEG_INF = -1e30
FORCED_SCORE = 1e9

COL_SIZES = (D_NSA, 2 * KV_NSA * HEAD_DIM, 2 * KV_NSA * HEAD_DIM, 2 * KV_NSA * HEAD_DIM,
             3 * H_NSA, D_NSA, 3 * D_FOX, H_FOX, D_FOX)
N_IN = sum(COL_SIZES)
COL_CUTS = tuple(int(c) for c in np.cumsum(COL_SIZES)[:-1])

kernel_name = 'hymba_nsa_fox_decode_step'


def rms_norm(x, g):
    xf = x.astype(jnp.float32)
    y = xf * lax.rsqrt(jnp.mean(xf * xf, axis=-1, keepdims=True) + RMS_EPS)
    return (y * g.astype(jnp.float32)).astype(x.dtype)


def masked_softmax(logits, mask):
    l = jnp.where(mask, logits.astype(jnp.float32), NEG_INF)
    m = jnp.max(l, axis=-1, keepdims=True)
    e = jnp.where(mask, jnp.exp(l - m), 0.0)
    return e / jnp.maximum(jnp.sum(e, axis=-1, keepdims=True), 1e-30)


def t5_bucket(rel):
    n = jnp.maximum(rel, 0)
    nf = jnp.maximum(n, T5_EXACT).astype(jnp.float32)
    large = T5_EXACT + (jnp.log(nf / T5_EXACT) / math.log(T5_MAX_DIST / T5_EXACT)
                        * (T5_BUCKETS - T5_EXACT)).astype(jnp.int32)
    return jnp.where(n < T5_EXACT, n, jnp.minimum(large, T5_BUCKETS - 1))


def gather_pages(cache, page_table, layer):
    rows = cache[page_table, layer]
    b, n_pages, page = rows.shape[:3]
    return rows.reshape((b, n_pages * page) + rows.shape[3:])


def project(h, w_in, b_fgate):
    b, t = h.shape[:2]
    u = h @ w_in
    q_n, kv_c, kv_s, kv_w, g_n, z_n, qkv_f, f_f, z_f = jnp.split(u, COL_CUTS, axis=-1)
    kv_shape = (b, t, 2, KV_NSA, HEAD_DIM)
    qkv_f = qkv_f.reshape(b, t, 3, H_FOX, HEAD_DIM)
    logf = jax.nn.log_sigmoid(f_f.astype(jnp.float32) + b_fgate.astype(jnp.float32))
    gates = jax.nn.sigmoid(g_n.reshape(b, t, KV_NSA, GRP, 3))
    return (q_n.reshape(b, t, KV_NSA, GRP, HEAD_DIM), kv_c.reshape(kv_shape), kv_s.reshape(kv_shape),
            kv_w.reshape(kv_shape), gates, z_n, qkv_f[:, :, 0], qkv_f[:, :, 1:], logf, z_f)


def compress_blocks(kv, cmp_pos, w_cmp1, w_cmp2):
    b, t = kv.shape[:2]
    n_sub = t // CMP_STRIDE
    sub = kv[:, :n_sub * CMP_STRIDE].reshape(b, n_sub, CMP_STRIDE, 2, KV_NSA, HEAD_DIM)
    blk = jnp.concatenate([sub[:, :-1], sub[:, 1:]], axis=2)
    blk = blk + cmp_pos.transpose(1, 0, 2)[:, :, None, :]
    flat = blk.transpose(0, 1, 3, 4, 2, 5).reshape(b, n_sub - 1, 2, KV_NSA, CMP_LEN * HEAD_DIM)
    hid = jax.nn.silu(jnp.einsum('bnekx,exf->bnekf', flat, w_cmp1))
    return jnp.einsum('bnekf,efd->bnekd', hid, w_cmp2)


def selection_blocks(kv):
    b, t = kv.shape[:2]
    n_sel = -(-t // SEL_BLOCK)
    kv = jnp.pad(kv, ((0, 0), (0, n_sel * SEL_BLOCK - t), (0, 0), (0, 0), (0, 0)))
    kv = kv.reshape(b, n_sel, SEL_BLOCK, 2, KV_NSA, HEAD_DIM).transpose(0, 4, 1, 2, 3, 5)
    return kv[..., 0, :], kv[..., 1, :]


def nsa_attend(q, q_pos, cmp_kv, slc_k, slc_v, win_kv, win_pos, gates, t5_table):
    b, tq = q.shape[:2]
    scale = HEAD_DIM ** -0.5
    tbl = t5_table.reshape(T5_BUCKETS, KV_NSA, GRP)
    n_cmp = cmp_kv.shape[1]
    end_pos = jnp.arange(n_cmp, dtype=jnp.int32) * CMP_STRIDE + CMP_LEN - 1
    rel_c = q_pos[:, None] - end_pos[None, :]
    bias_c = tbl[t5_bucket(rel_c)].astype(jnp.float32).transpose(0, 2, 3, 1)
    logit_c = jnp.einsum('bqghd,bngd->bqghn', q, cmp_kv[:, :, 0]).astype(jnp.float32) * scale + bias_c
    p_c = masked_softmax(logit_c, (rel_c >= 0)[:, None, None, :])
    o_c = jnp.einsum('bqghn,bngd->bqghd', p_c.astype(q.dtype), cmp_kv[:, :, 1])
    imp = p_c.sum(axis=3)
    pp = jnp.pad(imp, ((0, 0), (0, 0), (0, 0), (1, 1)))
    p_sub = pp[..., 1:] + pp[..., :-1]
    n_sel = slc_k.shape[2]
    p_sub = jnp.pad(p_sub, ((0, 0), (0, 0), (0, 0), (0, n_sel * SUB_PER_SEL - p_sub.shape[-1])))
    imp_sel = p_sub.reshape(p_sub.shape[:-1] + (n_sel, SUB_PER_SEL)).sum(-1)
    blk = jnp.arange(n_sel, dtype=jnp.int32)[None, :]
    cur = (q_pos // SEL_BLOCK)[:, None]
    forced = (blk == 0) | (blk == cur) | (blk == cur - 1)
    future = blk * SEL_BLOCK > q_pos[:, None]
    score = jnp.where(future[:, None], NEG_INF, jnp.where(forced[:, None], FORCED_SCORE, imp_sel))
    n_top = min(TOP_N, n_sel)
    _, idx = lax.top_k(score, n_top)
    idx = idx.transpose(0, 2, 1, 3)
    gather = jax.vmap(jax.vmap(lambda blocks, ib: blocks[ib]))
    k_sel = gather(slc_k, idx)
    v_sel = gather(slc_v, idx)
    pos_sel = idx[..., None] * SEL_BLOCK + jnp.arange(SEL_BLOCK, dtype=jnp.int32)
    rel_s = q_pos[None, None, :, None, None] - pos_sel
    bias_s = jax.vmap(lambda bk, tb: tb[bk], in_axes=(1, 1), out_axes=1)(t5_bucket(rel_s), tbl)
    bias_s = bias_s.astype(jnp.float32).transpose(0, 2, 1, 5, 3, 4).reshape(b, tq, KV_NSA, GRP, n_top * SEL_BLOCK)
    logit_s = jnp.einsum('bqghd,bgqncd->bqghnc', q, k_sel).reshape(b, tq, KV_NSA, GRP, n_top * SEL_BLOCK)
    logit_s = logit_s.astype(jnp.float32) * scale + bias_s
    mask_s = (rel_s >= 0).transpose(0, 2, 1, 3, 4).reshape(b, tq, KV_NSA, 1, n_top * SEL_BLOCK)
    p_s = masked_softmax(logit_s, mask_s).reshape(b, tq, KV_NSA, GRP, n_top, SEL_BLOCK)
    o_s = jnp.einsum('bqghnc,bgqncd->bqghd', p_s.astype(q.dtype), v_sel)
    rel_w = q_pos[:, None] - win_pos[None, :]
    bias_w = tbl[t5_bucket(rel_w)].astype(jnp.float32).transpose(0, 2, 3, 1)
    mask_w = ((rel_w >= 0) & (rel_w < WINDOW) & (win_pos[None, :] >= 0))[:, None, None, :]
    logit_w = jnp.einsum('bqghd,bsgd->bqghs', q, win_kv[:, :, 0]).astype(jnp.float32) * scale + bias_w
    p_w = masked_softmax(logit_w, mask_w)
    o_w = jnp.einsum('bqghs,bsgd->bqghd', p_w.astype(q.dtype), win_kv[:, :, 1])
    return gates[..., 0:1] * o_c + gates[..., 1:2] * o_s + gates[..., 2:3] * o_w


def fox_attend(q, cq, q_pos, k, v, ck, k_pos):
    logit = jnp.einsum('bqhd,bshd->bhqs', q, k).astype(jnp.float32) * (HEAD_DIM ** -0.5)
    logit = logit + (cq.transpose(0, 2, 1)[..., :, None] - ck.transpose(0, 2, 1)[..., None, :])
    p = masked_softmax(logit, k_pos[None, :] <= q_pos[:, None])
    return jnp.einsum('bhqs,bshd->bqhd', p.astype(q.dtype), v)


def mixer_out(x, o_n, z_n, o_f, z_f, g_post, w_out, p, w_pproj, w_pgate):
    b, t = x.shape[:2]
    mix = jnp.concatenate([o_n.reshape(b, t, D_NSA) * jax.nn.silu(z_n),
                           o_f.reshape(b, t, D_FOX) * jax.nn.silu(z_f)], axis=-1)
    x = x + rms_norm(mix @ w_out, g_post)
    return x + (p @ w_pproj) * jax.nn.sigmoid(x @ w_pgate)


def prompt_layer(x, p, g_pre, g_post, w_in, b_fgate, cmp_pos, w_cmp1, w_cmp2, w_out, w_pproj, w_pgate, t5_table):
    b, t = x.shape[:2]
    n_blk = t // Q_BLOCK
    q_n, kv_c, kv_s, kv_w, gates, z_n, q_f, kv_f, logf, z_f = project(rms_norm(x, g_pre), w_in, b_fgate)
    pos = jnp.arange(t, dtype=jnp.int32)
    cmp_kv = compress_blocks(kv_c, cmp_pos, w_cmp1, w_cmp2)
    slc_k, slc_v = selection_blocks(kv_s)
    win_pad = jnp.pad(kv_w, ((0, 0), (WINDOW, 0), (0, 0), (0, 0), (0, 0)))

    def nsa_block(args):
        i, qb, gb = args
        q0 = i * Q_BLOCK
        wkv = lax.dynamic_slice_in_dim(win_pad, q0, WINDOW + Q_BLOCK, axis=1)
        wpos = q0 - WINDOW + jnp.arange(WINDOW + Q_BLOCK, dtype=jnp.int32)
        qpos = q0 + jnp.arange(Q_BLOCK, dtype=jnp.int32)
        return nsa_attend(qb, qpos, cmp_kv, slc_k, slc_v, wkv, wpos, gb, t5_table)

    to_blocks = lambda a: a.reshape((b, n_blk, Q_BLOCK) + a.shape[2:]).swapaxes(0, 1)
    from_blocks = lambda a: a.swapaxes(0, 1).reshape((b, t) + a.shape[3:])
    o_n = from_blocks(lax.map(nsa_block, (jnp.arange(n_blk, dtype=jnp.int32), to_blocks(q_n), to_blocks(gates))))

    c = jnp.cumsum(logf, axis=1)
    k_f, v_f = kv_f[:, :, 0], kv_f[:, :, 1]

    def fox_block(args):
        qb, cb, pb = args
        return fox_attend(qb, cb, pb, k_f, v_f, c, pos)

    o_f = from_blocks(lax.map(fox_block, (to_blocks(q_f), to_blocks(c), pos.reshape(n_blk, Q_BLOCK))))
    y = mixer_out(x, o_n, z_n, o_f, z_f, g_post, w_out, p, w_pproj, w_pgate)
    w_keep = min(WINDOW, t)
    return y, (kv_c, kv_s, kv_f, logf, kv_w[:, t - w_keep:])


def sample_layer(x, p, cache_cmp_kv, cache_slc_kv, cache_fox_kv, cache_fox_logf, state_win_kv, page_table, layer,
                 g_pre, g_post, w_in, b_fgate, cmp_pos, w_cmp1, w_cmp2, w_out, w_pproj, w_pgate, t5_table):
    b, t = x.shape[:2]
    past_len = page_table.shape[1] * PAGE_SIZE
    q_n, kv_c, kv_s, kv_w, gates, z_n, q_f, kv_f, logf, z_f = project(rms_norm(x, g_pre), w_in, b_fgate)
    pos = past_len + jnp.arange(t, dtype=jnp.int32)
    full_c = jnp.concatenate([gather_pages(cache_cmp_kv, page_table, layer), kv_c], axis=1)
    cmp_kv = compress_blocks(full_c, cmp_pos, w_cmp1, w_cmp2)
    full_s = jnp.concatenate([gather_pages(cache_slc_kv, page_table, layer), kv_s], axis=1)
    slc_k, slc_v = selection_blocks(full_s)
    win_kv = jnp.concatenate([state_win_kv[:, layer], kv_w], axis=1)
    w_buf = state_win_kv.shape[2]
    win_pos = past_len - w_buf + jnp.arange(w_buf + t, dtype=jnp.int32)
    o_n = nsa_attend(q_n, pos, cmp_kv, slc_k, slc_v, win_kv, win_pos, gates, t5_table)
    past_kv = gather_pages(cache_fox_kv, page_table, layer)
    c_past = jnp.cumsum(gather_pages(cache_fox_logf, page_table, layer).astype(jnp.float32), axis=1)
    c_new = c_past[:, -1:] + jnp.cumsum(logf, axis=1)
    k_all = jnp.concatenate([past_kv[:, :, 0], kv_f[:, :, 0]], axis=1)
    v_all = jnp.concatenate([past_kv[:, :, 1], kv_f[:, :, 1]], axis=1)
    o_f = fox_attend(q_f, c_new, pos, k_all, v_all, jnp.concatenate([c_past, c_new], axis=1),
                     jnp.arange(past_len + t, dtype=jnp.int32))
    y = mixer_out(x, o_n, z_n, o_f, z_f, g_post, w_out, p, w_pproj, w_pgate)
    w_keep = min(WINDOW, past_len + t)
    return y, (kv_c, kv_s, kv_f, logf, win_kv[:, win_kv.shape[1] - w_keep:])


def setup_inputs(seed: int = 0) -> dict:
    key = jax.random.key(seed)
    ks = jax.random.split(key, 24)
    n_pages = PAST_LEN // PAGE_SIZE
    n_used = DEC_BATCH * n_pages
    n_pool = n_used + (n_used + 3) // 4
    w_buf = min(WINDOW, PAST_LEN)
    nrm = lambda k, shape, s=1.0: s * jax.random.normal(k, shape, jnp.float32)
    page_table = jax.random.permutation(ks[0], n_pool)[:n_used].reshape(DEC_BATCH, n_pages).astype(jnp.int32)
    return {
        'x_prompt': nrm(ks[1], (BATCH, SEQ, D_MODEL)),
        'x_sample': nrm(ks[2], (DEC_BATCH, DEC_SEQ, D_MODEL)),
        'cache_cmp_kv': nrm(ks[3], (n_pool, DEPTH, PAGE_SIZE, 2, KV_NSA, HEAD_DIM)),
        'cache_slc_kv': nrm(ks[4], (n_pool, DEPTH, PAGE_SIZE, 2, KV_NSA, HEAD_DIM)),
        'cache_fox_kv': nrm(ks[5], (n_pool, DEPTH, PAGE_SIZE, 2, H_FOX, HEAD_DIM)),
        'cache_fox_logf': jax.nn.log_sigmoid(FGATE_BIAS_INIT + nrm(ks[6], (n_pool, DEPTH, PAGE_SIZE, H_FOX))),
        'state_win_kv': nrm(ks[7], (DEC_BATCH, DEPTH, w_buf, 2, KV_NSA, HEAD_DIM)),
        'page_table': page_table,
        'p_prompt': nrm(ks[8], (DEPTH, BATCH, SEQ, PLE_DIM)),
        'p_sample': nrm(ks[9], (DEPTH, DEC_BATCH, DEC_SEQ, PLE_DIM)),
        'g_pre': 1.0 + nrm(ks[10], (DEPTH, D_MODEL), 0.05),
        'g_post': 1.0 + nrm(ks[11], (DEPTH, D_MODEL), 0.05),
        'w_in': nrm(ks[12], (DEPTH, D_MODEL, N_IN), D_MODEL ** -0.5),
        'b_fgate': FGATE_BIAS_INIT + nrm(ks[13], (DEPTH, H_FOX), 0.5),
        'cmp_pos': nrm(ks[14], (DEPTH, 2, CMP_LEN, HEAD_DIM), 0.1),
        'w_cmp1': nrm(ks[15], (DEPTH, 2, CMP_LEN * HEAD_DIM, CMP_HIDDEN), (CMP_LEN * HEAD_DIM) ** -0.5),
        'w_cmp2': nrm(ks[16], (DEPTH, 2, CMP_HIDDEN, HEAD_DIM), CMP_HIDDEN ** -0.5),
        'w_out': nrm(ks[17], (DEPTH, D_MODEL, D_MODEL), D_MODEL ** -0.5),
        'w_pproj': nrm(ks[18], (DEPTH, PLE_DIM, D_MODEL), PLE_DIM ** -0.5),
        'w_pgate': nrm(ks[19], (DEPTH, D_MODEL, D_MODEL), D_MODEL ** -0.5),
        't5_table': nrm(ks[20], (T5_BUCKETS, H_NSA), 0.5),
    }


def reference(x_prompt, x_sample, cache_cmp_kv, cache_slc_kv, cache_fox_kv, cache_fox_logf, state_win_kv,
              page_table, p_prompt, p_sample, g_pre, g_post, w_in, b_fgate, cmp_pos, w_cmp1, w_cmp2,
              w_out, w_pproj, w_pgate, t5_table):
    xp, xs = x_prompt, x_sample
    rows_p, rows_s = [], []
    for layer in range(DEPTH):
        lw = (g_pre[layer], g_post[layer], w_in[layer], b_fgate[layer], cmp_pos[layer], w_cmp1[layer],
              w_cmp2[layer], w_out[layer], w_pproj[layer], w_pgate[layer], t5_table)
        xp, r_p = prompt_layer(xp, p_prompt[layer], *lw)
        xs, r_s = sample_layer(xs, p_sample[layer], cache_cmp_kv, cache_slc_kv, cache_fox_kv, cache_fox_logf,
                               state_win_kv, page_table, layer, *lw)
        rows_p.append(r_p)
        rows_s.append(r_s)
    stack = lambda rows, i: jnp.stack([r[i] for r in rows], axis=1)
    return (xp, xs,
            stack(rows_p, 0), stack(rows_p, 1), stack(rows_p, 2), stack(rows_p, 3), stack(rows_p, 4),
            stack(rows_s, 0), stack(rows_s, 1), stack(rows_s, 2), stack(rows_s, 3), stack(rows_s, 4))
```

```python
import functools
import math

import numpy as np
import jax
import jax.numpy as jnp
from jax import lax
from jax.experimental import pallas as pl
from jax.experimental.pallas import tpu as pltpu

F32 = jnp.float32
BF16 = jnp.bfloat16

D_MODEL = 1024
HEAD_DIM = 64
KV_NSA = 2
GRP = 4
H_FOX = 8
D_NSA = 512
D_FOX = 512
CMP_STRIDE = 16
CMP_LEN = 32
CMP_HIDDEN = 128
SEL_BLOCK = 64
TOP_N = 16
WINDOW = 512
T5_BUCKETS = 32
T5_EXACT = 16
T5_MAX_DIST = 1024
PLE_DIM = 256
PAGE_SIZE = 128
RMS_EPS = 1e-6
NEG_INF = -1e30
FORCED_SCORE = 1e9

LANES = 128
TQ = 128
BK = 512
PAGES_PER_STEP = 8
VMEM_LIMIT = 56 * 1024 * 1024
N_PROJ = 3968

_NT = (((1,), (1,)), ((), ()))


def _cparams(sem):
    return pltpu.CompilerParams(dimension_semantics=sem, vmem_limit_bytes=VMEM_LIMIT)


def _sigmoid(x):
    return 1.0 / (1.0 + jnp.exp(-x))


def _proj_kernel(x_ref, g_ref, w_ref, bf_ref, kvc_ref, kvs_ref, kvw_ref, kvf_ref, small_ref, zn_ref, zf_ref,
                 qn_ref, qf_ref, kvsb_ref, kvwb_ref, kvfb_ref):
    x = x_ref[...]
    h = x * lax.rsqrt(jnp.mean(x * x, axis=-1, keepdims=True) + RMS_EPS) * g_ref[...]
    u = jnp.dot(h.astype(BF16), w_ref[...], preferred_element_type=F32)
    scale = HEAD_DIM ** -0.5
    qn_ref[...] = (u[:, 0:512] * scale).astype(BF16)
    kvc_ref[...] = u[:, 512:768]
    kvs = u[:, 768:1024]
    kvs_ref[...] = kvs
    kvsb_ref[...] = kvs.astype(BF16)
    kvw = u[:, 1024:1280]
    kvw_ref[...] = kvw
    kvwb_ref[...] = kvw.astype(BF16)
    zn_ref[...] = u[:, 1280:1792]
    qf_ref[...] = (u[:, 1792:2304] * scale).astype(BF16)
    kvf = u[:, 2304:3328]
    kvf_ref[...] = kvf
    kvfb_ref[...] = kvf.astype(BF16)
    zf_ref[...] = u[:, 3328:3840]
    s = u[:, 3840:3968]
    lane = lax.broadcasted_iota(jnp.int32, s.shape, 1)
    f = s + bf_ref[...]
    logf = jnp.minimum(f, 0.0) - jnp.log(1.0 + jnp.exp(-jnp.abs(f)))
    small_ref[...] = jnp.where(lane < 8, logf, jnp.where(lane < 32, _sigmoid(s), 0.0))


def _proj(x2d, g_pre, w_perm, bf_pad, bm=256):
    r = x2d.shape[0]
    row = lambda c: pl.BlockSpec((bm, c), lambda i: (i, 0))
    const = lambda a, b: pl.BlockSpec((a, b), lambda i: (0, 0))
    f32 = lambda c: jax.ShapeDtypeStruct((r, c), F32)
    b16 = lambda c: jax.ShapeDtypeStruct((r, c), BF16)
    return pl.pallas_call(
        _proj_kernel,
        grid=(r // bm,),
        in_specs=[row(D_MODEL), const(1, D_MODEL), const(D_MODEL, N_PROJ), const(1, LANES)],
        out_specs=[row(256), row(256), row(256), row(1024), row(LANES), row(512), row(512),
                   row(512), row(512), row(256), row(256), row(1024)],
        out_shape=[f32(256), f32(256), f32(256), f32(1024), f32(LANES), f32(512), f32(512),
                   b16(512), b16(512), b16(256), b16(256), b16(1024)],
        compiler_params=_cparams(("parallel",)),
    )(x2d, g_pre, w_perm, bf_pad)


def _split3(x):
    a = x.astype(BF16)
    r = x - a.astype(F32)
    b = r.astype(BF16)
    c = (r - b.astype(F32)).astype(BF16)
    return a, b, c


def _cumsum_kernel(x_ref, cum_ref, cumt_ref, carry_ref):
    @pl.when(pl.program_id(1) == 0)
    def _():
        carry_ref[...] = jnp.zeros_like(carry_ref)

    rows = lax.broadcasted_iota(jnp.int32, (LANES, LANES), 0)
    cols = lax.broadcasted_iota(jnp.int32, (LANES, LANES), 1)
    tri = jnp.where(rows >= cols, 1.0, 0.0).astype(BF16)
    a, b, c = _split3(x_ref[...])
    y = (jnp.dot(tri, a, preferred_element_type=F32) + jnp.dot(tri, b, preferred_element_type=F32)
         + jnp.dot(tri, c, preferred_element_type=F32)) + carry_ref[0:1, :]
    cum_ref[...] = y
    cumt_ref[...] = y.T
    carry_ref[...] = jnp.broadcast_to(y[LANES - 1:LANES, :], carry_ref.shape)


def _cumsum(small3d):
    b, t, _ = small3d.shape
    return pl.pallas_call(
        _cumsum_kernel,
        grid=(b, t // LANES),
        in_specs=[pl.BlockSpec((None, LANES, LANES), lambda bi, j: (bi, j, 0))],
        out_specs=[pl.BlockSpec((None, LANES, LANES), lambda bi, j: (bi, j, 0)),
                   pl.BlockSpec((None, LANES, LANES), lambda bi, j: (bi, 0, j))],
        out_shape=[jax.ShapeDtypeStruct((b, t, LANES), F32), jax.ShapeDtypeStruct((b, LANES, t), F32)],
        scratch_shapes=[pltpu.VMEM((8, LANES), F32)],
        compiler_params=_cparams(("parallel", "arbitrary")),
    )(small3d)


def _fox_kernel(q_ref, k_ref, v_ref, cum_ref, cumt_ref, o_ref, *, i_off):
    p = pl.program_id(1)
    ig = pl.program_id(2) + i_off
    q0 = ig * TQ
    lane = lax.broadcasted_iota(jnp.int32, (TQ, LANES), 1)
    lo = lane < 64
    qt = q_ref[...]
    zero = jnp.zeros_like(qt)
    qs = (jnp.where(lo, qt, zero), jnp.where(lo, zero, qt))
    cum = cum_ref[...]
    cqs = tuple(jnp.sum(jnp.where(lane == 2 * p + hh, cum, 0.0), axis=1, keepdims=True) for hh in range(2))
    diff = (lax.broadcasted_iota(jnp.int32, (TQ, BK), 1) - lax.broadcasted_iota(jnp.int32, (TQ, BK), 0))
    n_tiles = (q0 + TQ - 1) // BK + 1

    def body(j, carry):
        k0 = pl.multiple_of(j * BK, BK)
        kt = k_ref[pl.ds(k0, BK), :]
        vt = v_ref[pl.ds(k0, BK), :]
        causal = diff <= (q0 - k0)
        out = []
        for hh in range(2):
            m_prev, l_prev, acc = carry[hh]
            ck = cumt_ref[pl.ds(2 * p + hh, 1), pl.ds(k0, BK)]
            s = lax.dot_general(qs[hh], kt, _NT, preferred_element_type=F32)
            s = s + (cqs[hh] - ck)
            s = jnp.where(causal, s, NEG_INF)
            m_new = jnp.maximum(m_prev, jnp.max(s, axis=1, keepdims=True))
            alpha = jnp.exp(m_prev - m_new)
            pr = jnp.exp(s - jnp.tile(m_new, (1, BK // LANES)))
            l_new = alpha * l_prev + jnp.sum(pr, axis=1, keepdims=True)
            acc = alpha * acc + jnp.dot(pr.astype(BF16), vt, preferred_element_type=F32)
            out.append((m_new, l_new, acc))
        return tuple(out)

    init = tuple((jnp.full((TQ, LANES), NEG_INF, F32), jnp.zeros((TQ, LANES), F32), jnp.zeros((TQ, LANES), F32))
                 for _ in range(2))
    res = lax.fori_loop(0, n_tiles, body, init)
    o_ref[...] = jnp.where(lo, res[0][2] / res[0][1], res[1][2] / res[1][1])


def _fox(qf, kvf_bf, cum, cumt, i_off):
    b, tq_all, _ = qf.shape
    tk = kvf_bf.shape[1]
    nq = tq_all // TQ
    return pl.pallas_call(
        functools.partial(_fox_kernel, i_off=i_off),
        grid=(b, 4, nq),
        in_specs=[pl.BlockSpec((None, TQ, LANES), lambda bi, p, i: (bi, i, p)),
                  pl.BlockSpec((None, tk, LANES), lambda bi, p, i: (bi, 0, p)),
                  pl.BlockSpec((None, tk, LANES), lambda bi, p, i: (bi, 0, 4 + p)),
                  pl.BlockSpec((None, TQ, LANES), lambda bi, p, i: (bi, i + i_off, 0)),
                  pl.BlockSpec((None, 8, tk), lambda bi, p, i: (bi, 0, 0))],
        out_specs=pl.BlockSpec((None, TQ, LANES), lambda bi, p, i: (bi, i, p)),
        out_shape=jax.ShapeDtypeStruct((b, tq_all, D_FOX), F32),
        compiler_params=_cparams(("parallel", "parallel", "arbitrary")),
    )(qf, kvf_bf, kvf_bf, cum, cumt)


def _cmp1_kernel(x_ref, pos_ref, w0_ref, w1_ref, p_ref):
    x = x_ref[...]
    x0 = (x + pos_ref[0:1, :]).astype(BF16)
    x1 = (x + pos_ref[1:2, :]).astype(BF16)
    p_ref[:, 0:512] = jnp.dot(x0, w0_ref[...], preferred_element_type=F32)
    p_ref[:, 512:1024] = jnp.dot(x1, w1_ref[...], preferred_element_type=F32)


def _cmp1(xflat, pos2, w0, w1, bm=256):
    r = xflat.shape[0]
    bm = min(bm, r)
    return pl.pallas_call(
        _cmp1_kernel,
        grid=(r // bm,),
        in_specs=[pl.BlockSpec((bm, 4096), lambda i: (i, 0)), pl.BlockSpec((2, 4096), lambda i: (0, 0)),
                  pl.BlockSpec((4096, 512), lambda i: (0, 0)), pl.BlockSpec((4096, 512), lambda i: (0, 0))],
        out_specs=pl.BlockSpec((bm, 1024), lambda i: (i, 0)),
        out_shape=jax.ShapeDtypeStruct((r, 1024), F32),
        compiler_params=_cparams(("parallel",)),
    )(xflat, pos2, w0, w1)


def _cmp2_kernel(p_ref, w2k_ref, w2v_ref, ck_ref, cvt_ref, *, nsub):
    pre = p_ref[:, 0:512] + pltpu.roll(p_ref[:, 512:1024], nsub - 1, 0)
    hid = (pre * _sigmoid(pre)).astype(BF16)
    ck_ref[...] = jnp.dot(hid[:, 0:256], w2k_ref[...], preferred_element_type=F32).astype(BF16)
    cv = jnp.dot(hid[:, 256:512], w2v_ref[...], preferred_element_type=F32)
    cvt_ref[...] = cv.T.astype(BF16)


def _cmp2(p3d, w2k, w2v):
    b, nsub, _ = p3d.shape
    return pl.pallas_call(
        functools.partial(_cmp2_kernel, nsub=nsub),
        grid=(b,),
        in_specs=[pl.BlockSpec((None, nsub, 1024), lambda bi: (bi, 0, 0)),
                  pl.BlockSpec((256, LANES), lambda bi: (0, 0)), pl.BlockSpec((256, LANES), lambda bi: (0, 0))],
        out_specs=[pl.BlockSpec((None, nsub, LANES), lambda bi: (bi, 0, 0)),
                   pl.BlockSpec((None, LANES, nsub), lambda bi: (bi, 0, 0))],
        out_shape=[jax.ShapeDtypeStruct((b, nsub, LANES), BF16), jax.ShapeDtypeStruct((b, LANES, nsub), BF16)],
        compiler_params=_cparams(("parallel",)),
    )(p3d, w2k, w2v)


def _nsa_cmp_kernel(q_ref, ck_ref, cvt_ref, tc_ref, a_ref, oc_ref, msk_ref, *, i_off, nsub, nb):
    ig = pl.program_id(1) + i_off
    n_cmp = nsub - 1
    n_io = lax.broadcasted_iota(jnp.int32, (nsub, TQ), 0)
    t_io = lax.broadcasted_iota(jnp.int32, (nsub, TQ), 1)
    valid = ((TQ * ig + t_io - CMP_STRIDE * n_io - (CMP_LEN - 1)) >= 0) & (n_io < n_cmp)
    start = pl.multiple_of(nsub - 8 * ig, 8)
    lane = lax.broadcasted_iota(jnp.int32, (TQ, LANES), 1)
    sub_lo = lax.broadcasted_iota(jnp.int32, (LANES, TQ), 0) < 64
    ck = ck_ref[...]
    cvt = cvt_ref[...]
    imps = [jnp.zeros((nsub, TQ), F32), jnp.zeros((nsub, TQ), F32)]
    for h in range(GRP):
        qt = q_ref[:, LANES * h:LANES * (h + 1)]
        zero = jnp.zeros_like(qt)
        halves = []
        for g in range(KV_NSA):
            qm = jnp.where(lane < 64, qt, zero) if g == 0 else jnp.where(lane < 64, zero, qt)
            st = lax.dot_general(ck, qm, _NT, preferred_element_type=F32)
            st = st + tc_ref[4 * g + h, pl.ds(start, nsub), :]
            l = jnp.where(valid, st, NEG_INF)
            m = jnp.max(l, axis=0, keepdims=True)
            e = jnp.where(valid, jnp.exp(l - m), 0.0)
            pc = e / jnp.maximum(jnp.sum(e, axis=0, keepdims=True), 1e-30)
            imps[g] = imps[g] + pc
            halves.append(jnp.dot(cvt, pc.astype(BF16), preferred_element_type=F32))
        oc_ref[:, LANES * h:LANES * (h + 1)] = jnp.where(sub_lo, halves[0], halves[1]).T

    blk = lax.broadcasted_iota(jnp.int32, (nb, TQ), 0)
    qpos = TQ * ig + lax.broadcasted_iota(jnp.int32, (nb, TQ), 1)
    cur = qpos >> 6
    forced = (blk == 0) | (blk == cur) | (blk == cur - 1)
    future = blk * SEL_BLOCK > qpos
    amat = a_ref[...]
    for g in range(KV_NSA):
        a, b, c = _split3(imps[g])
        imp_sel = (jnp.dot(amat, a, preferred_element_type=F32) + jnp.dot(amat, b, preferred_element_type=F32)
                   + jnp.dot(amat, c, preferred_element_type=F32))
        score = jnp.where(future, NEG_INF, jnp.where(forced, FORCED_SCORE, imp_sel))

        def pick(_, carry):
            sc, sel = carry
            mx = jnp.max(sc, axis=0, keepdims=True)
            idx = jnp.min(jnp.where(sc == mx, blk, nb), axis=0, keepdims=True)
            hit = blk == idx
            return jnp.where(hit, -jnp.inf, sc), jnp.where(hit, 1.0, sel)

        _, sel = lax.fori_loop(0, TOP_N, pick, (score, jnp.zeros((nb, TQ), F32)))
        msk_ref[g] = sel.T.astype(BF16)


def _nsa_cmp(qn, ck, cvt, tc, amat, i_off):
    b, tq_all, _ = qn.shape
    nsub = ck.shape[1]
    nb = amat.shape[0]
    nq = tq_all // TQ
    return pl.pallas_call(
        functools.partial(_nsa_cmp_kernel, i_off=i_off, nsub=nsub, nb=nb),
        grid=(b, nq),
        in_specs=[pl.BlockSpec((None, TQ, D_NSA), lambda bi, i: (bi, i, 0)),
                  pl.BlockSpec((None, nsub, LANES), lambda bi, i: (bi, 0, 0)),
                  pl.BlockSpec((None, LANES, nsub), lambda bi, i: (bi, 0, 0)),
                  pl.BlockSpec((8, 2 * nsub, LANES), lambda bi, i: (0, 0, 0)),
                  pl.BlockSpec((nb, nsub), lambda bi, i: (0, 0))],
        out_specs=[pl.BlockSpec((None, TQ, D_NSA), lambda bi, i: (bi, i, 0)),
                   pl.BlockSpec((None, None, 2, TQ, nb), lambda bi, i: (bi, i, 0, 0, 0))],
        out_shape=[jax.ShapeDtypeStruct((b, tq_all, D_NSA), F32),
                   jax.ShapeDtypeStruct((b, nq, 2, TQ, nb), BF16)],
        compiler_params=_cparams(("parallel", "arbitrary")),
    )(qn, ck, cvt, tc, amat)


def _nsa_sw_kernel(q_ref, ks_ref, kw_ref, msk_ref, bts_ref, btw_ref, oc_ref, sm_ref, on_ref, *, i_off, w_off, nb):
    ig = pl.program_id(1) + i_off
    lane = lax.broadcasted_iota(jnp.int32, (TQ, LANES), 1)
    lo = lane < 64
    blk_row = lax.broadcasted_iota(jnp.int32, (nb, BK), 0)
    key_blk = lax.broadcasted_iota(jnp.int32, (nb, BK), 1) >> 6
    n_tiles = (TQ * ig + TQ - 1) // BK + 1
    sub = BK // LANES
    o_sel = {}
    o_win = {}
    for g in range(KV_NSA):
        qms = []
        for h in range(GRP):
            qt = q_ref[:, LANES * h:LANES * (h + 1)]
            zero = jnp.zeros_like(qt)
            qms.append(jnp.where(lo, qt, zero) if g == 0 else jnp.where(lo, zero, qt))
        mg = msk_ref[g]

        def body(j, carry):
            k0 = pl.multiple_of(j * BK, BK)
            kt = ks_ref[pl.ds(k0, BK), 0:LANES]
            vt = ks_ref[pl.ds(k0, BK), LANES:2 * LANES]
            expand = jnp.where(blk_row == key_blk + (BK // SEL_BLOCK) * j, 1.0, 0.0).astype(BF16)
            picked = jnp.dot(mg, expand, preferred_element_type=F32) > 0.5
            out = []
            for h in range(GRP):
                m_prev, l_prev, acc = carry[h]
                s = lax.dot_general(qms[h], kt, _NT, preferred_element_type=F32)
                parts = []
                for c in range(sub):
                    e = jnp.clip(ig - (sub * j + c) + 1, 0, 9)
                    parts.append(s[:, LANES * c:LANES * (c + 1)] + bts_ref[4 * g + h, e])
                s = jnp.where(picked, jnp.concatenate(parts, axis=1), NEG_INF)
                m_new = jnp.maximum(m_prev, jnp.max(s, axis=1, keepdims=True))
                alpha = jnp.exp(m_prev - m_new)
                pr = jnp.exp(s - jnp.tile(m_new, (1, sub)))
                l_new = alpha * l_prev + jnp.sum(pr, axis=1, keepdims=True)
                acc = alpha * acc + jnp.dot(pr.astype(BF16), vt, preferred_element_type=F32)
                out.append((m_new, l_new, acc))
            return tuple(out)

        init = tuple((jnp.full((TQ, LANES), NEG_INF, F32), jnp.zeros((TQ, LANES), F32),
                      jnp.zeros((TQ, LANES), F32)) for _ in range(GRP))
        res = lax.fori_loop(0, n_tiles, body, init)
        for h in range(GRP):
            o_sel[(g, h)] = res[h][2] / res[h][1]

        n_w = WINDOW // TQ + 1
        kts, vts, es = [], [], []
        for d in range(n_w):
            tw = ig - d - w_off
            r0 = pl.multiple_of(jnp.maximum(tw, 0) * TQ, TQ)
            kts.append(kw_ref[pl.ds(r0, TQ), 0:LANES])
            vts.append(kw_ref[pl.ds(r0, TQ), LANES:2 * LANES])
            es.append(jnp.where(tw >= 0, d, n_w))
        for h in range(GRP):
            ss = [lax.dot_general(qms[h], kts[d], _NT, preferred_element_type=F32) + btw_ref[4 * g + h, es[d]]
                  for d in range(n_w)]
            m = ss[0]
            for d in range(1, n_w):
                m = jnp.maximum(m, ss[d])
            m = jnp.max(m, axis=1, keepdims=True)
            l = jnp.zeros((TQ, 1), F32)
            acc = jnp.zeros((TQ, LANES), F32)
            for d in range(n_w):
                pr = jnp.exp(ss[d] - m)
                l = l + jnp.sum(pr, axis=1, keepdims=True)
                acc = acc + jnp.dot(pr.astype(BF16), vts[d], preferred_element_type=F32)
            o_win[(g, h)] = acc / l

    sm = sm_ref[...]
    for h in range(GRP):
        def gate(br):
            c0 = 8 + 3 * h + br
            c1 = 8 + 3 * GRP + 3 * h + br
            return jnp.where(lo, sm[:, c0:c0 + 1], sm[:, c1:c1 + 1])
        o_c = oc_ref[:, LANES * h:LANES * (h + 1)]
        o_s = jnp.where(lo, o_sel[(0, h)], o_sel[(1, h)])
        o_w = jnp.where(lo, o_win[(0, h)], o_win[(1, h)])
        on_ref[:, LANES * h:LANES * (h + 1)] = gate(0) * o_c + gate(1) * o_s + gate(2) * o_w


def _nsa_sw(qn, ks_bf, kw_bf, msk, bts, btw, oc, small, i_off, w_off):
    b, tq_all, _ = qn.shape
    tk = ks_bf.shape[1]
    tw = kw_bf.shape[1]
    nb = msk.shape[-1]
    nq = tq_all // TQ
    return pl.pallas_call(
        functools.partial(_nsa_sw_kernel, i_off=i_off, w_off=w_off, nb=nb),
        grid=(b, nq),
        in_specs=[pl.BlockSpec((None, TQ, D_NSA), lambda bi, i: (bi, i, 0)),
                  pl.BlockSpec((None, tk, 256), lambda bi, i: (bi, 0, 0)),
                  pl.BlockSpec((None, tw, 256), lambda bi, i: (bi, 0, 0)),
                  pl.BlockSpec((None, None, 2, TQ, nb), lambda bi, i: (bi, i, 0, 0, 0)),
                  pl.BlockSpec((8, 10, TQ, LANES), lambda bi, i: (0, 0, 0, 0)),
                  pl.BlockSpec((8, 6, TQ, LANES), lambda bi, i: (0, 0, 0, 0)),
                  pl.BlockSpec((None, TQ, D_NSA), lambda bi, i: (bi, i, 0)),
                  pl.BlockSpec((None, TQ, LANES), lambda bi, i: (bi, i, 0))],
        out_specs=pl.BlockSpec((None, TQ, D_NSA), lambda bi, i: (bi, i, 0)),
        out_shape=jax.ShapeDtypeStruct((b, tq_all, D_NSA), F32),
        compiler_params=_cparams(("parallel", "arbitrary")),
    )(qn, ks_bf, kw_bf, msk, bts, btw, oc, small)


def _out_kernel(x_ref, on_ref, zn_ref, of_ref, zf_ref, p_ref, wo_ref, gp_ref, wpp_ref, wpg_ref, y_ref):
    zn = zn_ref[...]
    zf = zf_ref[...]
    mix_n = (on_ref[...] * (zn * _sigmoid(zn))).astype(BF16)
    mix_f = (of_ref[...] * (zf * _sigmoid(zf))).astype(BF16)
    o = (jnp.dot(mix_n, wo_ref[0:512, :], preferred_element_type=F32)
         + jnp.dot(mix_f, wo_ref[512:1024, :], preferred_element_type=F32))
    r = o * lax.rsqrt(jnp.mean(o * o, axis=-1, keepdims=True) + RMS_EPS) * gp_ref[...]
    x1 = x_ref[...] + r
    ple = jnp.dot(p_ref[...].astype(BF16), wpp_ref[...], preferred_element_type=F32)
    gate = _sigmoid(jnp.dot(x1.astype(BF16), wpg_ref[...], preferred_element_type=F32))
    y_ref[...] = x1 + ple * gate


def _mix_out(x2d, on, zn, of, zf, p2d, wo, g_post, wpp, wpg, bm=256):
    r = x2d.shape[0]
    row = lambda c: pl.BlockSpec((bm, c), lambda i: (i, 0))
    const = lambda a, b: pl.BlockSpec((a, b), lambda i: (0, 0))
    return pl.pallas_call(
        _out_kernel,
        grid=(r // bm,),
        in_specs=[row(D_MODEL), row(512), row(512), row(512), row(512), row(PLE_DIM),
                  const(D_MODEL, D_MODEL), const(1, D_MODEL), const(PLE_DIM, D_MODEL), const(D_MODEL, D_MODEL)],
        out_specs=row(D_MODEL),
        out_shape=jax.ShapeDtypeStruct((r, D_MODEL), F32),
        compiler_params=_cparams(("parallel",)),
    )(x2d, on, zn, of, zf, p2d, wo, g_post, wpp, wpg)


def _paged_copy_kernel(pt_ref, *refs, n_groups, width, with_new):
    del pt_ref
    pages = refs[:PAGES_PER_STEP]
    o_ref = refs[-1]
    j = pl.program_id(1)
    rows = pages[0].shape[0]

    @pl.when(j < n_groups)
    def _():
        if width != o_ref.shape[1]:
            o_ref[...] = jnp.zeros_like(o_ref)
        for i, pg in enumerate(pages):
            o_ref[rows * i:rows * (i + 1), 0:width] = pg[...].astype(o_ref.dtype)

    if with_new:
        new_ref = refs[PAGES_PER_STEP]

        @pl.when(j == n_groups)
        def _():
            o_ref[...] = jnp.zeros_like(o_ref)
            o_ref[0:new_ref.shape[0], :] = new_ref[...].astype(o_ref.dtype)


def _paged_copy(cache3d, page_table, new_rows, out_width, out_dtype):
    b, n_pages = page_table.shape
    _, rows, width = cache3d.shape
    n_groups = n_pages // PAGES_PER_STEP
    with_new = new_rows is not None
    steps = n_groups + (1 if with_new else 0)
    grp_rows = PAGES_PER_STEP * rows

    def page_map(k):
        return lambda bi, j, pt: (pt[bi, jnp.minimum(j, n_groups - 1) * PAGES_PER_STEP + k], 0, 0)

    in_specs = [pl.BlockSpec((None, rows, width), page_map(k)) for k in range(PAGES_PER_STEP)]
    args = [cache3d] * PAGES_PER_STEP
    if with_new:
        in_specs.append(pl.BlockSpec((None,) + new_rows.shape[1:], lambda bi, j, pt: (bi, 0, 0)))
        args.append(new_rows)
    return pl.pallas_call(
        functools.partial(_paged_copy_kernel, n_groups=n_groups, width=width, with_new=with_new),
        grid_spec=pltpu.PrefetchScalarGridSpec(
            num_scalar_prefetch=1, grid=(b, steps), in_specs=in_specs,
            out_specs=pl.BlockSpec((None, grp_rows, out_width), lambda bi, j, pt: (bi, j, 0))),
        out_shape=jax.ShapeDtypeStruct((b, steps * grp_rows, out_width), out_dtype),
        compiler_params=_cparams(("parallel", "arbitrary")),
    )(page_table, *args)


def _t5_bucket(rel):
    n = jnp.maximum(rel, 0)
    nf = jnp.maximum(n, T5_EXACT).astype(jnp.float32)
    large = T5_EXACT + (jnp.log(nf / T5_EXACT) / math.log(T5_MAX_DIST / T5_EXACT)
                        * (T5_BUCKETS - T5_EXACT)).astype(jnp.int32)
    return jnp.where(n < T5_EXACT, n, jnp.minimum(large, T5_BUCKETS - 1))


def _bias_tables(t5_table, nsub):
    tbl = t5_table.astype(F32)
    ti = jnp.arange(TQ, dtype=jnp.int32)[:, None]
    kj = jnp.arange(LANES, dtype=jnp.int32)[None, :]
    rel = TQ * jnp.arange(8, dtype=jnp.int32)[:, None, None] + ti - kj
    b = jnp.moveaxis(tbl[_t5_bucket(rel)], -1, 0)
    far = tbl[T5_BUCKETS - 1][:, None, None, None]
    sel = jnp.where(rel >= 0, b - far, NEG_INF)
    bts = jnp.concatenate([jnp.full((8, 1, TQ, LANES), NEG_INF, F32), sel, jnp.zeros((8, 1, TQ, LANES), F32)], 1)
    n_w = WINDOW // TQ + 1
    win = jnp.where((rel[:n_w] >= 0) & (rel[:n_w] < WINDOW), b[:, :n_w], NEG_INF)
    btw = jnp.concatenate([win, jnp.full((8, 1, TQ, LANES), NEG_INF, F32)], 1)
    u = jnp.arange(2 * nsub, dtype=jnp.int32)[:, None]
    rel_c = jnp.arange(TQ, dtype=jnp.int32)[None, :] - CMP_STRIDE * (u - nsub) - (CMP_LEN - 1)
    tc = jnp.moveaxis(tbl[_t5_bucket(rel_c)], -1, 0)
    return bts, btw, tc


def _importance_matrix(nb, nsub):
    a = np.zeros((nb, nsub), np.float32)
    for blk in range(nb):
        for j in range(4 * blk, 4 * blk + 4):
            for n in (j, j - 1):
                if 0 <= n < nsub:
                    a[blk, n] += 1.0
    return jnp.asarray(a, BF16)


def _sigma_cols():
    return np.concatenate([np.arange(64) + 64 * (4 * (s % 2) + s // 2) for s in range(8)])


def _prep_weights(g_pre, g_post, w_in, b_fgate, cmp_pos, w_cmp1, w_cmp2, w_out, w_pproj, w_pgate):
    sig = _sigma_cols()
    cols = np.concatenate([sig, np.arange(512, 1280), 1304 + sig, np.arange(1816, 3352), np.arange(3360, 3872),
                           np.arange(3352, 3360), np.arange(1280, 1304)])
    w_perm = jnp.concatenate([w_in[:, cols], jnp.zeros((D_MODEL, N_PROJ - cols.size), w_in.dtype)], 1).astype(BF16)
    bf_pad = jnp.zeros((1, LANES), F32).at[0, :H_FOX].set(b_fgate.astype(F32))
    wo = jnp.concatenate([w_out[sig], w_out[512:]], 0).astype(BF16)
    w1r = w_cmp1.reshape(2, 2, CMP_STRIDE, HEAD_DIM, CMP_HIDDEN)
    eye = jnp.eye(2, dtype=w_cmp1.dtype)
    wa = jnp.einsum('easdf,ex,gy->asegdxyf', w1r, eye, eye).reshape(2, 4096, 512).astype(BF16)
    pos = cmp_pos.reshape(2, 2, CMP_STRIDE, 1, HEAD_DIM)
    pos2 = jnp.broadcast_to(pos.transpose(1, 2, 0, 3, 4), (2, CMP_STRIDE, 2, KV_NSA, HEAD_DIM)).reshape(2, 4096)
    w2 = []
    for e in range(2):
        blk = jnp.zeros((256, LANES), w_cmp2.dtype)
        blk = blk.at[0:128, 0:64].set(w_cmp2[e]).at[128:256, 64:128].set(w_cmp2[e])
        w2.append(blk.astype(BF16))
    return dict(w_perm=w_perm, bf_pad=bf_pad, g_pre=g_pre.reshape(1, D_MODEL).astype(F32),
                g_post=g_post.reshape(1, D_MODEL).astype(F32), wo=wo, wpp=w_pproj.astype(BF16),
                wpg=w_pgate.astype(BF16), wa0=wa[0], wa1=wa[1], pos2=pos2.astype(F32), w2k=w2[0], w2v=w2[1])


def _layer(x3d, p3d, wts, tables, amat, i_off, w_off, past):
    b, tq_all, _ = x3d.shape
    rows = b * tq_all
    (kvc, kvs, kvw, kvf, small, zn, zf, qn, qf, kvs_bf, kvw_bf, kvf_bf) = _proj(
        x3d.reshape(rows, D_MODEL), wts['g_pre'], wts['w_perm'], wts['bf_pad'])
    r3 = lambda a: a.reshape(b, tq_all, a.shape[-1])
    bts, btw, tc = tables
    if past is None:
        ks_all, kf_all, kw_all, lf_all = r3(kvs_bf), r3(kvf_bf), r3(kvw_bf), r3(small)
        xflat = kvc.reshape(rows // CMP_STRIDE, 4096)
        nsub = tq_all // CMP_STRIDE
    else:
        pt = past['page_table']
        ks_all = _paged_copy(past['slc'], pt, r3(kvs_bf), 256, BF16)
        kf_all = _paged_copy(past['fox'], pt, r3(kvf_bf), 1024, BF16)
        lf_all = _paged_copy(past['logf'], pt, r3(small), LANES, F32)
        cflat = _paged_copy(past['cmp'], pt, None, 4096, F32)
        nsub = cflat.shape[1]
        xflat = cflat.reshape(b * nsub, 4096)
        kw_all = jnp.concatenate([past['win'].astype(BF16), r3(kvw_bf)], axis=1)
    cum, cumt = _cumsum(lf_all)
    o_f = _fox(r3(qf), kf_all, cum, cumt, i_off)
    pflat = _cmp1(xflat, wts['pos2'], wts['wa0'], wts['wa1'])
    ck, cvt = _cmp2(pflat.reshape(b, nsub, 1024), wts['w2k'], wts['w2v'])
    o_c, msk = _nsa_cmp(r3(qn), ck, cvt, tc, amat, i_off)
    o_n = _nsa_sw(r3(qn), ks_all, kw_all, msk, bts, btw, o_c, r3(small), i_off, w_off)
    y = _mix_out(x3d.reshape(rows, D_MODEL), o_n.reshape(rows, 512), zn, o_f.reshape(rows, 512), zf,
                 p3d.reshape(rows, PLE_DIM), wts['wo'], wts['g_post'], wts['wpp'], wts['wpg'])
    return y.reshape(b, tq_all, D_MODEL), r3(kvc), r3(kvs), r3(kvf), r3(small)[..., :H_FOX], r3(kvw)


def kernel(x_prompt, x_sample, cache_cmp_kv, cache_slc_kv, cache_fox_kv, cache_fox_logf, state_win_kv, page_table, p_prompt, p_sample, g_pre, g_post, w_in, b_fgate, cmp_pos, w_cmp1, w_cmp2, w_out, w_pproj, w_pgate, t5_table):
    depth = g_pre.shape[0]
    assert depth == 1, "single-layer trunk"
    bp, t, _ = x_prompt.shape
    bs, ts, _ = x_sample.shape
    n_pool = cache_cmp_kv.shape[0]
    n_pages = page_table.shape[1]
    past_len = n_pages * PAGE_SIZE
    w_buf = state_win_kv.shape[2]
    assert t % BK == 0 and past_len % (PAGES_PER_STEP * PAGE_SIZE) == 0 and w_buf == WINDOW and ts <= TQ

    wts = _prep_weights(g_pre[0], g_post[0], w_in[0], b_fgate[0], cmp_pos[0], w_cmp1[0], w_cmp2[0], w_out[0],
                        w_pproj[0], w_pgate[0])
    kv5 = lambda a, n: a.reshape(a.shape[0], 1, a.shape[1], 2, n, HEAD_DIM)

    nsub_p = t // CMP_STRIDE
    nb_p = -(-(t // SEL_BLOCK) // LANES) * LANES
    yp, kvc, kvs, kvf, logf, kvw = _layer(x_prompt, p_prompt[0], wts, _bias_tables(t5_table, nsub_p),
                                          _importance_matrix(nb_p, nsub_p), 0, 0, None)
    w_keep = min(WINDOW, t)
    outs_p = (kv5(kvc, KV_NSA), kv5(kvs, KV_NSA), kv5(kvf, H_FOX), logf[:, None], kv5(kvw[:, t - w_keep:], KV_NSA))

    pad = lambda a: jnp.pad(a, ((0, 0), (0, TQ - ts), (0, 0)))
    past = dict(page_table=page_table,
                cmp=cache_cmp_kv[:, 0].reshape(n_pool, PAGE_SIZE // CMP_STRIDE, 4096),
                slc=cache_slc_kv[:, 0].reshape(n_pool, PAGE_SIZE, 256),
                fox=cache_fox_kv[:, 0].reshape(n_pool, PAGE_SIZE, 1024),
                logf=cache_fox_logf[:, 0],
                win=state_win_kv[:, 0].reshape(bs, w_buf, 256))
    nsub_s = past_len // CMP_STRIDE
    nb_s = -(-(past_len // SEL_BLOCK + 2) // LANES) * LANES
    ys, kvc_s, kvs_s, kvf_s, logf_s, kvw_s = _layer(
        pad(x_sample), pad(p_sample[0]), wts, _bias_tables(t5_table, nsub_s), _importance_matrix(nb_s, nsub_s),
        past_len // TQ, (past_len - w_buf) // TQ, past)
    win_new = jnp.concatenate([past['win'], kvw_s[:, :ts]], axis=1)
    w_keep_s = min(WINDOW, past_len + ts)
    outs_s = (kv5(kvc_s[:, :ts], KV_NSA), kv5(kvs_s[:, :ts], KV_NSA), kv5(kvf_s[:, :ts], H_FOX),
              logf_s[:, None, :ts], kv5(win_new[:, win_new.shape[1] - w_keep_s:], KV_NSA))
    return (yp, ys[:, :ts]) + outs_p + outs_s
```

```python
import functools
import math

import numpy as np
import jax
import jax.numpy as jnp
from jax import lax
from jax.experimental import pallas as pl
from jax.experimental.pallas import tpu as pltpu

F32 = jnp.float32
BF16 = jnp.bfloat16

D_MODEL = 1024
HEAD_DIM = 64
KV_NSA = 2
GRP = 4
H_FOX = 8
D_NSA = 512
D_FOX = 512
CMP_STRIDE = 16
CMP_LEN = 32
CMP_HIDDEN = 128
SEL_BLOCK = 64
TOP_N = 16
WINDOW = 512
T5_BUCKETS = 32
T5_EXACT = 16
T5_MAX_DIST = 1024
PLE_DIM = 256
PAGE_SIZE = 128
RMS_EPS = 1e-6
NEG_INF = -1e30
FORCED_SCORE = 1e9

LANES = 128
SUBLANES = 8
TQ = 128
BK = 512
CUM_ROWS = 1024
PAGES_PER_STEP = 8
VMEM_LIMIT = 56 * 1024 * 1024
N_PROJ = 3968
BIAS_SPAN = 1536

_NT = (((1,), (1,)), ((), ()))


def _cparams(sem):
    return pltpu.CompilerParams(dimension_semantics=sem, vmem_limit_bytes=VMEM_LIMIT)


def _sigmoid(x):
    return 1.0 / (1.0 + jnp.exp(-x))


def _split3(x):
    a = x.astype(BF16)
    r = x - a.astype(F32)
    b = r.astype(BF16)
    c = (r - b.astype(F32)).astype(BF16)
    return a, b, c


def _dot3(lhs_parts, rhs):
    return sum(jnp.dot(p, rhs, preferred_element_type=F32) for p in lhs_parts)


def _proj_kernel(x_ref, g_ref, w_ref, bf_ref, kvc_ref, kvs_ref, kvw_ref, kvf_ref, small_ref, zn_ref, zf_ref,
                 qn_ref, qf_ref, kvsb_ref, kvwb_ref, kvfb_ref):
    x = x_ref[...]
    h = x * lax.rsqrt(jnp.mean(x * x, axis=-1, keepdims=True) + RMS_EPS) * g_ref[...]
    u = jnp.dot(h.astype(BF16), w_ref[...], preferred_element_type=F32)
    scale = HEAD_DIM ** -0.5
    qn_ref[...] = (u[:, 0:512] * scale).astype(BF16)
    kvc_ref[...] = u[:, 512:768]
    kvs = u[:, 768:1024]
    kvs_ref[...] = kvs
    kvsb_ref[...] = kvs.astype(BF16)
    kvw = u[:, 1024:1280]
    kvw_ref[...] = kvw
    kvwb_ref[...] = kvw.astype(BF16)
    zn_ref[...] = u[:, 1280:1792]
    qf_ref[...] = (u[:, 1792:2304] * scale).astype(BF16)
    kvf = u[:, 2304:3328]
    kvf_ref[...] = kvf
    kvfb_ref[...] = kvf.astype(BF16)
    zf_ref[...] = u[:, 3328:3840]
    s = u[:, 3840:3968]
    lane = lax.broadcasted_iota(jnp.int32, s.shape, 1)
    f = s + bf_ref[...]
    logf = jnp.minimum(f, 0.0) - jnp.log(1.0 + jnp.exp(-jnp.abs(f)))
    small_ref[...] = jnp.where(lane < 8, logf, jnp.where(lane < 32, _sigmoid(s), 0.0))


def _proj(x2d, g_pre, w_perm, bf_pad, bm=256):
    r = x2d.shape[0]
    bm = min(bm, r)
    row = lambda c: pl.BlockSpec((bm, c), lambda i: (i, 0))
    const = lambda a, b: pl.BlockSpec((a, b), lambda i: (0, 0))
    f32 = lambda c: jax.ShapeDtypeStruct((r, c), F32)
    b16 = lambda c: jax.ShapeDtypeStruct((r, c), BF16)
    return pl.pallas_call(
        _proj_kernel,
        grid=(r // bm,),
        in_specs=[row(D_MODEL), const(1, D_MODEL), const(D_MODEL, N_PROJ), const(1, LANES)],
        out_specs=[row(256), row(256), row(256), row(1024), row(LANES), row(512), row(512),
                   row(512), row(512), row(256), row(256), row(1024)],
        out_shape=[f32(256), f32(256), f32(256), f32(1024), f32(LANES), f32(512), f32(512),
                   b16(512), b16(512), b16(256), b16(256), b16(1024)],
        compiler_params=_cparams(("parallel",)),
    )(x2d, g_pre, w_perm, bf_pad)


def _cumsum_kernel(x_ref, cum_ref, cumt_ref, carry_ref):
    @pl.when(pl.program_id(1) == 0)
    def _():
        carry_ref[...] = jnp.zeros_like(carry_ref)

    rows = lax.broadcasted_iota(jnp.int32, (LANES, LANES), 0)
    cols = lax.broadcasted_iota(jnp.int32, (LANES, LANES), 1)
    tri = jnp.where(rows >= cols, 1.0, 0.0).astype(BF16)
    carry = carry_ref[0:1, :]
    for s in range(CUM_ROWS // LANES):
        sl = slice(LANES * s, LANES * (s + 1))
        a, b, c = _split3(x_ref[sl, :])
        y = (jnp.dot(tri, a, preferred_element_type=F32) + jnp.dot(tri, b, preferred_element_type=F32)
             + jnp.dot(tri, c, preferred_element_type=F32)) + carry
        cum_ref[sl, :] = y
        cumt_ref[:, sl] = y.T
        carry = y[LANES - 1:LANES, :]
    carry_ref[...] = jnp.broadcast_to(carry, carry_ref.shape)


def _cumsum(small3d):
    b, t, _ = small3d.shape
    return pl.pallas_call(
        _cumsum_kernel,
        grid=(b, t // CUM_ROWS),
        in_specs=[pl.BlockSpec((None, CUM_ROWS, LANES), lambda bi, j: (bi, j, 0))],
        out_specs=[pl.BlockSpec((None, CUM_ROWS, LANES), lambda bi, j: (bi, j, 0)),
                   pl.BlockSpec((None, LANES, CUM_ROWS), lambda bi, j: (bi, 0, j))],
        out_shape=[jax.ShapeDtypeStruct((b, t, LANES), F32), jax.ShapeDtypeStruct((b, LANES, t), F32)],
        scratch_shapes=[pltpu.VMEM((SUBLANES, LANES), F32)],
        compiler_params=_cparams(("parallel", "arbitrary")),
    )(small3d)


def _fox_kernel(q_ref, k_ref, v_ref, cum_ref, cumt_ref, o_ref):
    pp = pl.program_id(1)
    q0 = pl.program_id(2) * TQ
    lane = lax.broadcasted_iota(jnp.int32, (TQ, LANES), 1)
    lo = lane < 64
    cum = cum_ref[...]
    qs, cqs = [], []
    for a in range(2):
        qt = q_ref[:, LANES * a:LANES * (a + 1)]
        zero = jnp.zeros_like(qt)
        for hh in range(2):
            qs.append(jnp.where(lo, qt, zero) if hh == 0 else jnp.where(lo, zero, qt))
            head = 4 * pp + 2 * a + hh
            cqs.append(jnp.sum(jnp.where(lane == head, cum, 0.0), axis=1, keepdims=True))
    diff = (lax.broadcasted_iota(jnp.int32, (TQ, BK), 1) - lax.broadcasted_iota(jnp.int32, (TQ, BK), 0))
    n_full = (q0 + 1) // BK

    def tile(j, carry, masked):
        k0 = pl.multiple_of(j * BK, BK)
        kts = [k_ref[pl.ds(k0, BK), LANES * a:LANES * (a + 1)] for a in range(2)]
        vts = [v_ref[pl.ds(k0, BK), LANES * a:LANES * (a + 1)] for a in range(2)]
        ss = []
        for n in range(4):
            ck = cumt_ref[pl.ds(4 * pp + n, 1), pl.ds(k0, BK)]
            s = lax.dot_general(qs[n], kts[n // 2], _NT, preferred_element_type=F32) + (cqs[n] - ck)
            ss.append(jnp.where(diff <= (q0 - k0), s, NEG_INF) if masked else s)
        stats, prs = [], []
        for n in range(4):
            m_prev, l_prev, _ = carry[n]
            m_new = jnp.maximum(m_prev, jnp.max(ss[n], axis=1, keepdims=True))
            alpha = jnp.exp(m_prev - m_new)
            pr = jnp.exp(ss[n] - jnp.tile(m_new, (1, BK // LANES)))
            stats.append((m_new, alpha * l_prev + jnp.sum(pr, axis=1, keepdims=True), alpha))
            prs.append(pr.astype(BF16))
        return tuple((stats[n][0], stats[n][1],
                      stats[n][2] * carry[n][2] + jnp.dot(prs[n], vts[n // 2], preferred_element_type=F32))
                     for n in range(4))

    init = tuple((jnp.full((TQ, LANES), NEG_INF, F32), jnp.zeros((TQ, LANES), F32), jnp.zeros((TQ, LANES), F32))
                 for _ in range(4))
    carry = lax.fori_loop(0, n_full, lambda j, c: tile(j, c, False), init)
    res = tile(n_full, carry, True)
    for a in range(2):
        o_ref[:, LANES * a:LANES * (a + 1)] = jnp.where(lo, res[2 * a][2] / res[2 * a][1],
                                                        res[2 * a + 1][2] / res[2 * a + 1][1])


def _fox(qf, kvf_bf, cum, cumt):
    b, t, _ = qf.shape
    return pl.pallas_call(
        _fox_kernel,
        grid=(b, 2, t // TQ),
        in_specs=[pl.BlockSpec((None, TQ, 256), lambda bi, p, i: (bi, i, p)),
                  pl.BlockSpec((None, t, 256), lambda bi, p, i: (bi, 0, p)),
                  pl.BlockSpec((None, t, 256), lambda bi, p, i: (bi, 0, 2 + p)),
                  pl.BlockSpec((None, TQ, LANES), lambda bi, p, i: (bi, i, 0)),
                  pl.BlockSpec((None, SUBLANES, t), lambda bi, p, i: (bi, 0, 0))],
        out_specs=pl.BlockSpec((None, TQ, 256), lambda bi, p, i: (bi, i, p)),
        out_shape=jax.ShapeDtypeStruct((b, t, D_FOX), F32),
        compiler_params=_cparams(("parallel", "parallel", "arbitrary")),
    )(qf, kvf_bf, kvf_bf, cum, cumt)


def _cmp1_kernel(x_ref, pos_ref, w0_ref, w1_ref, p_ref):
    x = x_ref[...]
    x0 = (x + pos_ref[0:1, :]).astype(BF16)
    x1 = (x + pos_ref[1:2, :]).astype(BF16)
    p_ref[:, 0:512] = jnp.dot(x0, w0_ref[...], preferred_element_type=F32)
    p_ref[:, 512:1024] = jnp.dot(x1, w1_ref[...], preferred_element_type=F32)


def _cmp1(xflat, pos2, w0, w1, bm=256):
    r = xflat.shape[0]
    bm = min(bm, r)
    return pl.pallas_call(
        _cmp1_kernel,
        grid=(r // bm,),
        in_specs=[pl.BlockSpec((bm, 4096), lambda i: (i, 0)), pl.BlockSpec((2, 4096), lambda i: (0, 0)),
                  pl.BlockSpec((4096, 512), lambda i: (0, 0)), pl.BlockSpec((4096, 512), lambda i: (0, 0))],
        out_specs=pl.BlockSpec((bm, 1024), lambda i: (i, 0)),
        out_shape=jax.ShapeDtypeStruct((r, 1024), F32),
        compiler_params=_cparams(("parallel",)),
    )(xflat, pos2, w0, w1)


def _cmp2_kernel(p_ref, w2k_ref, w2v_ref, ck_ref, cvt_ref, *, nsub):
    pre = p_ref[:, 0:512] + pltpu.roll(p_ref[:, 512:1024], nsub - 1, 0)
    hid = (pre * _sigmoid(pre)).astype(BF16)
    ck_ref[...] = jnp.dot(hid[:, 0:256], w2k_ref[...], preferred_element_type=F32).astype(BF16)
    cv = jnp.dot(hid[:, 256:512], w2v_ref[...], preferred_element_type=F32)
    cvt_ref[...] = cv.T.astype(BF16)


def _cmp2(p3d, w2k, w2v):
    b, nsub, _ = p3d.shape
    return pl.pallas_call(
        functools.partial(_cmp2_kernel, nsub=nsub),
        grid=(b,),
        in_specs=[pl.BlockSpec((None, nsub, 1024), lambda bi: (bi, 0, 0)),
                  pl.BlockSpec((256, LANES), lambda bi: (0, 0)), pl.BlockSpec((256, LANES), lambda bi: (0, 0))],
        out_specs=[pl.BlockSpec((None, nsub, LANES), lambda bi: (bi, 0, 0)),
                   pl.BlockSpec((None, LANES, nsub), lambda bi: (bi, 0, 0))],
        out_shape=[jax.ShapeDtypeStruct((b, nsub, LANES), BF16), jax.ShapeDtypeStruct((b, LANES, nsub), BF16)],
        compiler_params=_cparams(("parallel",)),
    )(p3d, w2k, w2v)


def _nsa_cmp_kernel(q_ref, ck_ref, cvt_ref, tc_ref, a_ref, oc_ref, msk_ref, *, i_off, nsub, nb):
    ig = pl.program_id(1) + i_off
    n_cmp = nsub - 1
    n_io = lax.broadcasted_iota(jnp.int32, (nsub, TQ), 0)
    t_io = lax.broadcasted_iota(jnp.int32, (nsub, TQ), 1)
    valid = ((TQ * ig + t_io - CMP_STRIDE * n_io - (CMP_LEN - 1)) >= 0) & (n_io < n_cmp)
    start = pl.multiple_of(nsub - 8 * ig, 8)
    lane = lax.broadcasted_iota(jnp.int32, (TQ, LANES), 1)
    sub_lo = lax.broadcasted_iota(jnp.int32, (LANES, TQ), 0) < 64
    ck = ck_ref[...]
    cvt = cvt_ref[...]
    imps = [jnp.zeros((nsub, TQ), F32), jnp.zeros((nsub, TQ), F32)]
    for h in range(GRP):
        qt = q_ref[:, LANES * h:LANES * (h + 1)]
        zero = jnp.zeros_like(qt)
        halves = []
        for g in range(KV_NSA):
            qm = jnp.where(lane < 64, qt, zero) if g == 0 else jnp.where(lane < 64, zero, qt)
            st = lax.dot_general(ck, qm, _NT, preferred_element_type=F32)
            st = st + tc_ref[4 * g + h, pl.ds(start, nsub), :]
            l = jnp.where(valid, st, NEG_INF)
            m = jnp.max(l, axis=0, keepdims=True)
            e = jnp.where(valid, jnp.exp(l - m), 0.0)
            pc = e / jnp.maximum(jnp.sum(e, axis=0, keepdims=True), 1e-30)
            imps[g] = imps[g] + pc
            halves.append(jnp.dot(cvt, pc.astype(BF16), preferred_element_type=F32))
        oc_ref[:, LANES * h:LANES * (h + 1)] = jnp.where(sub_lo, halves[0], halves[1]).T

    blk = lax.broadcasted_iota(jnp.int32, (nb, TQ), 0)
    qpos = TQ * ig + lax.broadcasted_iota(jnp.int32, (nb, TQ), 1)
    cur = qpos >> 6
    forced = (blk == 0) | (blk == cur) | (blk == cur - 1)
    future = blk * SEL_BLOCK > qpos
    amat = a_ref[...]
    for g in range(KV_NSA):
        a, b, c = _split3(imps[g])
        imp_sel = (jnp.dot(amat, a, preferred_element_type=F32) + jnp.dot(amat, b, preferred_element_type=F32)
                   + jnp.dot(amat, c, preferred_element_type=F32))
        score = jnp.where(future, NEG_INF, jnp.where(forced, FORCED_SCORE, imp_sel))

        def pick(_, carry):
            sc, sel = carry
            mx = jnp.max(sc, axis=0, keepdims=True)
            idx = jnp.min(jnp.where(sc == mx, blk, nb), axis=0, keepdims=True)
            hit = blk == idx
            return jnp.where(hit, -jnp.inf, sc), jnp.where(hit, 1.0, sel)

        _, sel = lax.fori_loop(0, TOP_N, pick, (score, jnp.zeros((nb, TQ), F32)))
        msk_ref[g] = sel.T.astype(BF16)


def _nsa_cmp(qn, ck, cvt, tc, amat, i_off):
    b, tq_all, _ = qn.shape
    nsub = ck.shape[1]
    nb = amat.shape[0]
    nq = tq_all // TQ
    return pl.pallas_call(
        functools.partial(_nsa_cmp_kernel, i_off=i_off, nsub=nsub, nb=nb),
        grid=(b, nq),
        in_specs=[pl.BlockSpec((None, TQ, D_NSA), lambda bi, i: (bi, i, 0)),
                  pl.BlockSpec((None, nsub, LANES), lambda bi, i: (bi, 0, 0)),
                  pl.BlockSpec((None, LANES, nsub), lambda bi, i: (bi, 0, 0)),
                  pl.BlockSpec((8, 2 * nsub, LANES), lambda bi, i: (0, 0, 0)),
                  pl.BlockSpec((nb, nsub), lambda bi, i: (0, 0))],
        out_specs=[pl.BlockSpec((None, TQ, D_NSA), lambda bi, i: (bi, i, 0)),
                   pl.BlockSpec((None, None, 2, TQ, nb), lambda bi, i: (bi, i, 0, 0, 0))],
        out_shape=[jax.ShapeDtypeStruct((b, tq_all, D_NSA), F32),
                   jax.ShapeDtypeStruct((b, nq, 2, TQ, nb), BF16)],
        compiler_params=_cparams(("parallel", "arbitrary")),
    )(qn, ck, cvt, tc, amat)


def _nsa_sw_kernel(q_ref, ks_ref, kw_ref, msk_ref, bts_ref, btw_ref, oc_ref, sm_ref, on_ref, *, nb):
    ig = pl.program_id(1)
    lane = lax.broadcasted_iota(jnp.int32, (TQ, LANES), 1)
    lo = lane < 64
    blk_row = lax.broadcasted_iota(jnp.int32, (nb, BK), 0)
    key_blk = lax.broadcasted_iota(jnp.int32, (nb, BK), 1) >> 6
    n_tiles = (TQ * ig + TQ - 1) // BK + 1
    sub = BK // LANES
    o_sel = {}
    o_win = {}
    for g in range(KV_NSA):
        qms = []
        for h in range(GRP):
            qt = q_ref[:, LANES * h:LANES * (h + 1)]
            zero = jnp.zeros_like(qt)
            qms.append(jnp.where(lo, qt, zero) if g == 0 else jnp.where(lo, zero, qt))
        mg = msk_ref[g]

        def body(j, carry):
            k0 = pl.multiple_of(j * BK, BK)
            kt = ks_ref[pl.ds(k0, BK), 0:LANES]
            vt = ks_ref[pl.ds(k0, BK), LANES:2 * LANES]
            expand = jnp.where(blk_row == key_blk + (BK // SEL_BLOCK) * j, 1.0, 0.0).astype(BF16)
            picked = jnp.dot(mg, expand, preferred_element_type=F32) > 0.5
            es = [jnp.clip(ig - (sub * j + c) + 1, 0, 9) for c in range(sub)]
            ss = []
            for h in range(GRP):
                s = lax.dot_general(qms[h], kt, _NT, preferred_element_type=F32)
                parts = [s[:, LANES * c:LANES * (c + 1)] + bts_ref[4 * g + h, es[c]] for c in range(sub)]
                ss.append(jnp.where(picked, jnp.concatenate(parts, axis=1), NEG_INF))
            stats, prs = [], []
            for h in range(GRP):
                m_prev, l_prev, _ = carry[h]
                m_new = jnp.maximum(m_prev, jnp.max(ss[h], axis=1, keepdims=True))
                alpha = jnp.exp(m_prev - m_new)
                pr = jnp.exp(ss[h] - jnp.tile(m_new, (1, sub)))
                stats.append((m_new, alpha * l_prev + jnp.sum(pr, axis=1, keepdims=True), alpha))
                prs.append(pr.astype(BF16))
            return tuple((stats[h][0], stats[h][1],
                          stats[h][2] * carry[h][2] + jnp.dot(prs[h], vt, preferred_element_type=F32))
                         for h in range(GRP))

        init = tuple((jnp.full((TQ, LANES), NEG_INF, F32), jnp.zeros((TQ, LANES), F32),
                      jnp.zeros((TQ, LANES), F32)) for _ in range(GRP))
        res = lax.fori_loop(0, n_tiles, body, init)
        for h in range(GRP):
            o_sel[(g, h)] = res[h][2] / res[h][1]

        n_w = WINDOW // TQ + 1
        kts, vts, es = [], [], []
        for d in range(n_w):
            tw = ig - d
            r0 = pl.multiple_of(jnp.maximum(tw, 0) * TQ, TQ)
            kts.append(kw_ref[pl.ds(r0, TQ), 0:LANES])
            vts.append(kw_ref[pl.ds(r0, TQ), LANES:2 * LANES])
            es.append(jnp.where(tw >= 0, d, n_w))
        for h in range(GRP):
            ss = [lax.dot_general(qms[h], kts[d], _NT, preferred_element_type=F32) + btw_ref[4 * g + h, es[d]]
                  for d in range(n_w)]
            m = ss[0]
            for d in range(1, n_w):
                m = jnp.maximum(m, ss[d])
            m = jnp.max(m, axis=1, keepdims=True)
            l = jnp.zeros((TQ, 1), F32)
            acc = jnp.zeros((TQ, LANES), F32)
            for d in range(n_w):
                pr = jnp.exp(ss[d] - m)
                l = l + jnp.sum(pr, axis=1, keepdims=True)
                acc = acc + jnp.dot(pr.astype(BF16), vts[d], preferred_element_type=F32)
            o_win[(g, h)] = acc / l

    sm = sm_ref[...]
    for h in range(GRP):
        def gate(br):
            c0 = 8 + 3 * h + br
            c1 = 8 + 3 * GRP + 3 * h + br
            return jnp.where(lo, sm[:, c0:c0 + 1], sm[:, c1:c1 + 1])
        o_c = oc_ref[:, LANES * h:LANES * (h + 1)]
        o_s = jnp.where(lo, o_sel[(0, h)], o_sel[(1, h)])
        o_w = jnp.where(lo, o_win[(0, h)], o_win[(1, h)])
        on_ref[:, LANES * h:LANES * (h + 1)] = gate(0) * o_c + gate(1) * o_s + gate(2) * o_w


def _nsa_sw(qn, ks_bf, kw_bf, msk, bts, btw, oc, small):
    b, t, _ = qn.shape
    nb = msk.shape[-1]
    return pl.pallas_call(
        functools.partial(_nsa_sw_kernel, nb=nb),
        grid=(b, t // TQ),
        in_specs=[pl.BlockSpec((None, TQ, D_NSA), lambda bi, i: (bi, i, 0)),
                  pl.BlockSpec((None, t, 256), lambda bi, i: (bi, 0, 0)),
                  pl.BlockSpec((None, t, 256), lambda bi, i: (bi, 0, 0)),
                  pl.BlockSpec((None, None, 2, TQ, nb), lambda bi, i: (bi, i, 0, 0, 0)),
                  pl.BlockSpec((8, 10, TQ, LANES), lambda bi, i: (0, 0, 0, 0)),
                  pl.BlockSpec((8, 6, TQ, LANES), lambda bi, i: (0, 0, 0, 0)),
                  pl.BlockSpec((None, TQ, D_NSA), lambda bi, i: (bi, i, 0)),
                  pl.BlockSpec((None, TQ, LANES), lambda bi, i: (bi, i, 0))],
        out_specs=pl.BlockSpec((None, TQ, D_NSA), lambda bi, i: (bi, i, 0)),
        out_shape=jax.ShapeDtypeStruct((b, t, D_NSA), F32),
        compiler_params=_cparams(("parallel", "arbitrary")),
    )(qn, ks_bf, kw_bf, msk, bts, btw, oc, small)


def _out_kernel(x_ref, on_ref, zn_ref, of_ref, zf_ref, p_ref, wo_ref, gp_ref, wpp_ref, wpg_ref, y_ref):
    zn = zn_ref[...]
    zf = zf_ref[...]
    mix_n = (on_ref[...] * (zn * _sigmoid(zn))).astype(BF16)
    mix_f = (of_ref[...] * (zf * _sigmoid(zf))).astype(BF16)
    o = (jnp.dot(mix_n, wo_ref[0:512, :], preferred_element_type=F32)
         + jnp.dot(mix_f, wo_ref[512:1024, :], preferred_element_type=F32))
    r = o * lax.rsqrt(jnp.mean(o * o, axis=-1, keepdims=True) + RMS_EPS) * gp_ref[...]
    x1 = x_ref[...] + r
    ple = jnp.dot(p_ref[...].astype(BF16), wpp_ref[...], preferred_element_type=F32)
    gate = _sigmoid(jnp.dot(x1.astype(BF16), wpg_ref[...], preferred_element_type=F32))
    y_ref[...] = x1 + ple * gate


def _mix_out(x2d, on, zn, of, zf, p2d, wo, g_post, wpp, wpg, bm=256):
    r = x2d.shape[0]
    bm = min(bm, r)
    row = lambda c: pl.BlockSpec((bm, c), lambda i: (i, 0))
    const = lambda a, b: pl.BlockSpec((a, b), lambda i: (0, 0))
    return pl.pallas_call(
        _out_kernel,
        grid=(r // bm,),
        in_specs=[row(D_MODEL), row(512), row(512), row(512), row(512), row(PLE_DIM),
                  const(D_MODEL, D_MODEL), const(1, D_MODEL), const(PLE_DIM, D_MODEL), const(D_MODEL, D_MODEL)],
        out_specs=row(D_MODEL),
        out_shape=jax.ShapeDtypeStruct((r, D_MODEL), F32),
        compiler_params=_cparams(("parallel",)),
    )(x2d, on, zn, of, zf, p2d, wo, g_post, wpp, wpg)


def _page_specs(block_rows, block_cols, n_groups):
    def page_map(k):
        return lambda bi, j, pt: (pt[bi, jnp.minimum(j, n_groups - 1) * PAGES_PER_STEP + k], 0, 0)
    return [pl.BlockSpec((None, block_rows, block_cols), page_map(k)) for k in range(PAGES_PER_STEP)]


def _paged_rows_kernel(pt_ref, *refs):
    del pt_ref
    o_ref = refs[-1]
    rows = refs[0].shape[0]
    for i, pg in enumerate(refs[:PAGES_PER_STEP]):
        o_ref[rows * i:rows * (i + 1), :] = pg[...]


def _paged_rows(cache3d, page_table):
    b, n_pages = page_table.shape
    _, rows, width = cache3d.shape
    n_groups = n_pages // PAGES_PER_STEP
    grp_rows = PAGES_PER_STEP * rows
    return pl.pallas_call(
        _paged_rows_kernel,
        grid_spec=pltpu.PrefetchScalarGridSpec(
            num_scalar_prefetch=1, grid=(b, n_groups), in_specs=_page_specs(rows, width, n_groups),
            out_specs=pl.BlockSpec((None, grp_rows, width), lambda bi, j, pt: (bi, j, 0))),
        out_shape=jax.ShapeDtypeStruct((b, n_groups * grp_rows, width), cache3d.dtype),
        compiler_params=_cparams(("parallel", "arbitrary")),
    )(page_table, *([cache3d] * PAGES_PER_STEP))


def _logf_cumsum_kernel(pt_ref, *refs, n_groups):
    del pt_ref
    pages = refs[:PAGES_PER_STEP]
    new_ref, out_ref, xs_ref, carry_ref = refs[PAGES_PER_STEP:]
    j = pl.program_id(1)

    @pl.when(j == 0)
    def _():
        carry_ref[...] = jnp.zeros_like(carry_ref)

    @pl.when(j < n_groups)
    def _():
        for i, pg in enumerate(pages):
            xs_ref[SUBLANES * i:SUBLANES * (i + 1), :] = pg[...]

    @pl.when(j == n_groups)
    def _():
        xs_ref[...] = jnp.zeros_like(xs_ref)
        xs_ref[0:SUBLANES, :] = new_ref[...]

    rows = lax.broadcasted_iota(jnp.int32, (LANES, LANES), 0)
    cols = lax.broadcasted_iota(jnp.int32, (LANES, LANES), 1)
    triu = jnp.where(rows <= cols, 1.0, 0.0).astype(BF16)
    y = _dot3(_split3(xs_ref[...]), triu)
    carry = carry_ref[...]
    for i in range(PAGES_PER_STEP):
        yi = y[SUBLANES * i:SUBLANES * (i + 1), :] + carry
        out_ref[i] = yi
        carry = jnp.broadcast_to(yi[:, LANES - 1:LANES], (SUBLANES, LANES))
    carry_ref[...] = carry


def _logf_cumsum(logf_t, page_table, new_t):
    b, n_pages = page_table.shape
    n_groups = n_pages // PAGES_PER_STEP
    in_specs = _page_specs(H_FOX, PAGE_SIZE, n_groups)
    in_specs.append(pl.BlockSpec((None, H_FOX, LANES), lambda bi, j, pt: (bi, 0, 0)))
    return pl.pallas_call(
        functools.partial(_logf_cumsum_kernel, n_groups=n_groups),
        grid_spec=pltpu.PrefetchScalarGridSpec(
            num_scalar_prefetch=1, grid=(b, n_groups + 1), in_specs=in_specs,
            out_specs=pl.BlockSpec((None, PAGES_PER_STEP, H_FOX, LANES), lambda bi, j, pt: (bi, j, 0, 0)),
            scratch_shapes=[pltpu.VMEM((PAGES_PER_STEP * SUBLANES, LANES), F32),
                            pltpu.VMEM((SUBLANES, LANES), F32)]),
        out_shape=jax.ShapeDtypeStruct((b, (n_groups + 1) * PAGES_PER_STEP, H_FOX, LANES), F32),
        compiler_params=_cparams(("parallel", "arbitrary")),
    )(page_table, *([logf_t] * PAGES_PER_STEP), new_t)


def _fill_keys(j, n_groups, pages, new_ref, kv_scr):
    @pl.when(j < n_groups)
    def _():
        for i, pg in enumerate(pages):
            kv_scr[:, LANES * i:LANES * (i + 1)] = pg[...].astype(BF16)

    @pl.when(j == n_groups)
    def _():
        kv_scr[...] = jnp.zeros_like(kv_scr)
        kv_scr[:, 0:LANES] = new_ref[...]


def _online_update(s, m_scr, l_scr, acc_scr, v_t):
    nk = s.shape[1]
    m_prev = m_scr[...]
    m_new = jnp.maximum(m_prev, jnp.max(s, axis=1, keepdims=True))
    alpha = jnp.exp(m_prev - m_new)
    pr = jnp.exp(s - jnp.tile(m_new, (1, nk // LANES)))
    l_scr[...] = alpha * l_scr[...] + jnp.sum(pr, axis=1, keepdims=True)
    pv = lax.dot_general(pr.astype(BF16), v_t, _NT, preferred_element_type=F32)
    acc_scr[...] = jnp.tile(alpha, (1, acc_scr.shape[1] // LANES)) * acc_scr[...] + pv
    m_scr[...] = m_new


def _fox_sample_kernel(pt_ref, *refs, n_groups, ts):
    del pt_ref
    pages = refs[:PAGES_PER_STEP]
    new_ref, ck_ref, q_ref, cq_ref, o_ref, kv_scr, m_scr, l_scr, acc_scr = refs[PAGES_PER_STEP:]
    j = pl.program_id(1)
    nk = PAGES_PER_STEP * PAGE_SIZE
    nr = ts * H_FOX

    @pl.when(j == 0)
    def _():
        m_scr[...] = jnp.full_like(m_scr, NEG_INF)
        l_scr[...] = jnp.zeros_like(l_scr)
        acc_scr[...] = jnp.zeros_like(acc_scr)

    _fill_keys(j, n_groups, pages, new_ref, kv_scr)
    s = jnp.dot(q_ref[...], kv_scr[0:D_FOX, :], preferred_element_type=F32)
    ck = jnp.concatenate([jnp.tile(ck_ref[i], (ts, 1)) for i in range(PAGES_PER_STEP)], axis=1)
    s = s + (jnp.tile(cq_ref[...], (1, PAGES_PER_STEP)) - ck)
    lane = lax.broadcasted_iota(jnp.int32, (nr, nk), 1)
    row_tok = lax.broadcasted_iota(jnp.int32, (nr, nk), 0) >> 3
    limit = jnp.where(j < n_groups, nk, row_tok)
    s = jnp.where(lane <= limit, s, NEG_INF)
    _online_update(s, m_scr, l_scr, acc_scr, kv_scr[D_FOX:2 * D_FOX, :])

    @pl.when(j == n_groups)
    def _():
        o = acc_scr[...] / jnp.tile(l_scr[...], (1, D_FOX // LANES))
        own = ((lax.broadcasted_iota(jnp.int32, (nr, D_FOX), 0) & 7)
               == (lax.broadcasted_iota(jnp.int32, (nr, D_FOX), 1) >> 6))
        o = jnp.where(own, o, 0.0)
        rows = [jnp.sum(o[H_FOX * t:H_FOX * (t + 1)], axis=0, keepdims=True) for t in range(ts)]
        o_ref[...] = jnp.concatenate(rows + [jnp.zeros((SUBLANES - ts, D_FOX), F32)], axis=0)


def _fox_sample(fox_t, page_table, new_t, ck_t, qbd, cq, ts):
    b, n_pages = page_table.shape
    n_groups = n_pages // PAGES_PER_STEP
    nk = PAGES_PER_STEP * PAGE_SIZE
    nr = ts * H_FOX
    fixed = lambda shape: pl.BlockSpec((None,) + shape, lambda bi, j, pt: (bi,) + (0,) * len(shape))
    in_specs = _page_specs(2 * D_FOX, PAGE_SIZE, n_groups)
    in_specs += [fixed((2 * D_FOX, LANES)),
                 pl.BlockSpec((None, PAGES_PER_STEP, H_FOX, LANES), lambda bi, j, pt: (bi, j, 0, 0)),
                 fixed((nr, D_FOX)), fixed((nr, LANES))]
    return pl.pallas_call(
        functools.partial(_fox_sample_kernel, n_groups=n_groups, ts=ts),
        grid_spec=pltpu.PrefetchScalarGridSpec(
            num_scalar_prefetch=1, grid=(b, n_groups + 1), in_specs=in_specs,
            out_specs=fixed((SUBLANES, D_FOX)),
            scratch_shapes=[pltpu.VMEM((2 * D_FOX, nk), BF16), pltpu.VMEM((nr, LANES), F32),
                            pltpu.VMEM((nr, LANES), F32), pltpu.VMEM((nr, D_FOX), F32)]),
        out_shape=jax.ShapeDtypeStruct((b, SUBLANES, D_FOX), F32),
        compiler_params=_cparams(("parallel", "arbitrary")),
    )(page_table, *([fox_t] * PAGES_PER_STEP), new_t, ck_t, qbd, cq)


def _nsa_sample_kernel(pt_ref, *refs, n_groups, nb):
    del pt_ref
    pages = refs[:PAGES_PER_STEP]
    (new_ref, q_ref, msk_ref, bias_ref, wkv_ref, wnew_ref, biasw_ref, oc_ref, gate_ref, o_ref,
     kv_scr, m_scr, l_scr, acc_scr) = refs[PAGES_PER_STEP:]
    j = pl.program_id(1)
    nk = PAGES_PER_STEP * PAGE_SIZE

    @pl.when(j == 0)
    def _():
        m_scr[...] = jnp.full_like(m_scr, NEG_INF)
        l_scr[...] = jnp.zeros_like(l_scr)
        acc_scr[...] = jnp.zeros_like(acc_scr)

    _fill_keys(j, n_groups, pages, new_ref, kv_scr)
    q = q_ref[...]
    s = jnp.dot(q, kv_scr[0:LANES, :], preferred_element_type=F32) + bias_ref[...]
    blk_row = lax.broadcasted_iota(jnp.int32, (nb, nk), 0)
    key_blk = lax.broadcasted_iota(jnp.int32, (nb, nk), 1) >> 6
    expand = jnp.where(blk_row == key_blk + (nk // SEL_BLOCK) * j, 1.0, 0.0).astype(BF16)
    picked = jnp.dot(msk_ref[...], expand, preferred_element_type=F32) > 0.5
    s = jnp.where(picked, s, NEG_INF)
    _online_update(s, m_scr, l_scr, acc_scr, kv_scr[LANES:2 * LANES, :])

    @pl.when(j == n_groups)
    def _():
        o_s = acc_scr[...] / l_scr[...]
        wkv = wkv_ref[...].astype(BF16)
        wnew = wnew_ref[...]
        k_w = jnp.concatenate([wkv[0:LANES], wnew[0:LANES]], axis=1)
        v_w = jnp.concatenate([wkv[LANES:2 * LANES], wnew[LANES:2 * LANES]], axis=1)
        sw = jnp.dot(q, k_w, preferred_element_type=F32) + biasw_ref[...]
        pw = jnp.exp(sw - jnp.max(sw, axis=1, keepdims=True))
        o_w = (lax.dot_general(pw.astype(BF16), v_w, _NT, preferred_element_type=F32)
               / jnp.sum(pw, axis=1, keepdims=True))
        gt = gate_ref[...]
        o_ref[...] = gt[:, 0:1] * oc_ref[...] + gt[:, 1:2] * o_s + gt[:, 2:3] * o_w


def _nsa_sample(slc_t, page_table, new_t, q32, msk32, bias_s, win_t, wnew_t, bias_w, oc32, gate32):
    b, n_pages = page_table.shape
    n_groups = n_pages // PAGES_PER_STEP
    nk = PAGES_PER_STEP * PAGE_SIZE
    nr = q32.shape[1]
    nb = msk32.shape[-1]
    w_buf = win_t.shape[-1]
    fixed = lambda shape: pl.BlockSpec((None,) + shape, lambda bi, j, pt: (bi,) + (0,) * len(shape))
    in_specs = _page_specs(256, PAGE_SIZE, n_groups)
    in_specs += [fixed((256, LANES)), fixed((nr, LANES)), fixed((nr, nb)),
                 pl.BlockSpec((nr, nk), lambda bi, j, pt: (0, j)),
                 fixed((256, w_buf)), fixed((256, LANES)),
                 pl.BlockSpec((nr, w_buf + LANES), lambda bi, j, pt: (0, 0)),
                 fixed((nr, LANES)), fixed((nr, LANES))]
    return pl.pallas_call(
        functools.partial(_nsa_sample_kernel, n_groups=n_groups, nb=nb),
        grid_spec=pltpu.PrefetchScalarGridSpec(
            num_scalar_prefetch=1, grid=(b, n_groups + 1), in_specs=in_specs,
            out_specs=fixed((nr, LANES)),
            scratch_shapes=[pltpu.VMEM((256, nk), BF16), pltpu.VMEM((nr, LANES), F32),
                            pltpu.VMEM((nr, LANES), F32), pltpu.VMEM((nr, LANES), F32)]),
        out_shape=jax.ShapeDtypeStruct((b, nr, LANES), F32),
        compiler_params=_cparams(("parallel", "arbitrary")),
    )(page_table, *([slc_t] * PAGES_PER_STEP), new_t, q32, msk32, bias_s, win_t, wnew_t, bias_w, oc32, gate32)


def _t5_bucket(rel):
    n = jnp.maximum(rel, 0)
    nf = jnp.maximum(n, T5_EXACT).astype(jnp.float32)
    large = T5_EXACT + (jnp.log(nf / T5_EXACT) / math.log(T5_MAX_DIST / T5_EXACT)
                        * (T5_BUCKETS - T5_EXACT)).astype(jnp.int32)
    return jnp.where(n < T5_EXACT, n, jnp.minimum(large, T5_BUCKETS - 1))


def _bias_1d(t5_table):
    tbl = t5_table.astype(F32)
    f1d = tbl[_t5_bucket(jnp.arange(BIAS_SPAN, dtype=jnp.int32))].T
    return f1d, tbl[T5_BUCKETS - 1][:, None]


def _hankel(v, n_rows, n_cols):
    length = v.shape[1]
    assert length >= n_rows + n_cols
    flat = jnp.tile(v, (1, n_rows + 1))[:, :n_rows * (length + 1)]
    return flat.reshape(v.shape[0], n_rows, length + 1)[:, :, :n_cols]


def _neg(*shape):
    return jnp.full(shape, NEG_INF, F32)


def _prompt_tables(t5_table, nsub):
    f1d, far = _bias_1d(t5_table)
    nh = f1d.shape[0]
    span = TQ - 1 + 8 * LANES + LANES
    sel1 = jnp.concatenate([_neg(nh, TQ - 1), (f1d - far)[:, :span - (TQ - 1)]], axis=1)
    win1 = jnp.concatenate([_neg(nh, TQ - 1), f1d[:, :WINDOW], _neg(nh, span - (TQ - 1) - WINDOW)], axis=1)

    def tiles(v):
        h = _hankel(v, TQ, 8 * LANES).reshape(nh, TQ, 8, LANES)
        return jnp.flip(h, axis=3).transpose(0, 2, 1, 3)

    bts = jnp.concatenate([_neg(nh, 1, TQ, LANES), tiles(sel1), jnp.zeros((nh, 1, TQ, LANES), F32)], axis=1)
    n_w = WINDOW // TQ + 1
    btw = jnp.concatenate([tiles(win1)[:, :n_w], _neg(nh, 1, TQ, LANES)], axis=1)
    lead = CMP_LEN - 1 + 8 * CMP_STRIDE
    g2 = jnp.concatenate([jnp.broadcast_to(f1d[:, :1], (nh, lead)), f1d[:, :BIAS_SPAN - lead]], axis=1)
    n_r = 81
    h2 = _hankel(g2, TQ, CMP_STRIDE * (n_r - 1) + 1)[:, :, ::CMP_STRIDE]
    mid = jnp.flip(h2.transpose(0, 2, 1), axis=1)
    n_top = nsub - (n_r - 9)
    if n_top < 0:
        mid, n_top = mid[:, -n_top:], 0
    tc = jnp.concatenate([jnp.broadcast_to(far[:, :, None], (nh, n_top, TQ)), mid,
                          jnp.zeros((nh, 2 * nsub - n_top - mid.shape[1], TQ), F32)], axis=1)
    return bts, btw, tc


def _sample_tables(t5_table, past_len, ts, w_buf, n_keys):
    f1d, far = _bias_1d(t5_table)
    nh = f1d.shape[0]
    sel_rows, win_rows = [], []
    for tok in range(ts):
        n_vis = past_len + tok + 1
        near = min(BIAS_SPAN, n_vis)
        sel_rows.append(jnp.concatenate([jnp.zeros((nh, n_vis - near), F32),
                                         jnp.flip((f1d - far)[:, :near], axis=1),
                                         _neg(nh, n_keys - n_vis)], axis=1))
        win_rows.append(jnp.concatenate([_neg(nh, tok + 1 + w_buf - WINDOW), jnp.flip(f1d[:, :WINDOW], axis=1),
                                         _neg(nh, LANES - tok - 1)], axis=1))
    stack = lambda rows: jnp.stack(rows, axis=0).reshape(ts * nh, -1)
    return stack(sel_rows), stack(win_rows)


def _importance_matrix(nb, nsub):
    a = np.zeros((nb, nsub), np.float32)
    for blk in range(nb):
        for j in range(4 * blk, 4 * blk + 4):
            for n in (j, j - 1):
                if 0 <= n < nsub:
                    a[blk, n] += 1.0
    return jnp.asarray(a, BF16)


def _sigma_cols():
    return np.concatenate([np.arange(64) + 64 * (4 * (s % 2) + s // 2) for s in range(8)])


def _prep_weights(g_pre, g_post, w_in, b_fgate, cmp_pos, w_cmp1, w_cmp2, w_out, w_pproj, w_pgate):
    sig = _sigma_cols()
    cols = np.concatenate([sig, np.arange(512, 1280), 1304 + sig, np.arange(1816, 3352), np.arange(3360, 3872),
                           np.arange(3352, 3360), np.arange(1280, 1304)])
    w_perm = jnp.concatenate([w_in[:, cols], jnp.zeros((D_MODEL, N_PROJ - cols.size), w_in.dtype)], 1).astype(BF16)
    bf_pad = jnp.zeros((1, LANES), F32).at[0, :H_FOX].set(b_fgate.astype(F32))
    wo = jnp.concatenate([w_out[sig], w_out[512:]], 0).astype(BF16)
    w1r = w_cmp1.reshape(2, 2, CMP_STRIDE, HEAD_DIM, CMP_HIDDEN)
    eye = jnp.eye(2, dtype=w_cmp1.dtype)
    wa = jnp.einsum('easdf,ex,gy->asegdxyf', w1r, eye, eye).reshape(2, 4096, 512).astype(BF16)
    pos = cmp_pos.reshape(2, 2, CMP_STRIDE, 1, HEAD_DIM)
    pos2 = jnp.broadcast_to(pos.transpose(1, 2, 0, 3, 4), (2, CMP_STRIDE, 2, KV_NSA, HEAD_DIM)).reshape(2, 4096)
    w2 = []
    for e in range(2):
        blk = jnp.zeros((256, LANES), w_cmp2.dtype)
        blk = blk.at[0:128, 0:64].set(w_cmp2[e]).at[128:256, 64:128].set(w_cmp2[e])
        w2.append(blk.astype(BF16))
    return dict(w_perm=w_perm, bf_pad=bf_pad, g_pre=g_pre.reshape(1, D_MODEL).astype(F32),
                g_post=g_post.reshape(1, D_MODEL).astype(F32), wo=wo, wpp=w_pproj.astype(BF16),
                wpg=w_pgate.astype(BF16), wa0=wa[0], wa1=wa[1], pos2=pos2.astype(F32), w2k=w2[0], w2v=w2[1])


def _compressed_branch(qn3, xflat, b, nsub, wts, tc, amat, i_off):
    pflat = _cmp1(xflat, wts['pos2'], wts['wa0'], wts['wa1'])
    ck, cvt = _cmp2(pflat.reshape(b, nsub, 1024), wts['w2k'], wts['w2v'])
    return _nsa_cmp(qn3, ck, cvt, tc, amat, i_off)


def _prompt_group(x3d, p3d, wts, t5_table):
    b, t, _ = x3d.shape
    rows = b * t
    (kvc, kvs, kvw, kvf, small, zn, zf, qn, qf, kvs_bf, kvw_bf, kvf_bf) = _proj(
        x3d.reshape(rows, D_MODEL), wts['g_pre'], wts['w_perm'], wts['bf_pad'])
    r3 = lambda a: a.reshape(b, t, a.shape[-1])
    nsub = t // CMP_STRIDE
    nb = -(-(t // SEL_BLOCK) // LANES) * LANES
    bts, btw, tc = _prompt_tables(t5_table, nsub)
    cum, cumt = _cumsum(r3(small))
    o_f = _fox(r3(qf), r3(kvf_bf), cum, cumt)
    o_c, msk = _compressed_branch(r3(qn), kvc.reshape(rows // CMP_STRIDE, 4096), b, nsub, wts, tc,
                                  _importance_matrix(nb, nsub), 0)
    o_n = _nsa_sw(r3(qn), r3(kvs_bf), r3(kvw_bf), msk, bts, btw, o_c, r3(small))
    y = _mix_out(x3d.reshape(rows, D_MODEL), o_n.reshape(rows, 512), zn, o_f.reshape(rows, 512), zf,
                 p3d.reshape(rows, PLE_DIM), wts['wo'], wts['g_post'], wts['wpp'], wts['wpg'])
    return y.reshape(b, t, D_MODEL), r3(kvc), r3(kvs), r3(kvf), r3(small)[..., :H_FOX], r3(kvw)


def _token_minor(a, ts):
    return jnp.pad(jnp.swapaxes(a[:, :ts], 1, 2), ((0, 0), (0, 0), (0, LANES - ts)))


def _sample_group(x3d, p3d, wts, t5_table, caches, page_table):
    b, ts, _ = x3d.shape
    n_pages = page_table.shape[1]
    past_len = n_pages * PAGE_SIZE
    n_keys = past_len + PAGES_PER_STEP * PAGE_SIZE
    pad_rows = lambda a, n: jnp.pad(a, ((0, 0), (0, n - a.shape[1]), (0, 0)))
    (kvc, kvs, kvw, kvf, small, zn, zf, qn, qf, kvs_bf, kvw_bf, kvf_bf) = _proj(
        pad_rows(x3d, TQ).reshape(b * TQ, D_MODEL), wts['g_pre'], wts['w_perm'], wts['bf_pad'])
    r3 = lambda a: a.reshape(b, TQ, a.shape[-1])
    kvc, kvs, kvw, kvf, small, zn, zf, qn, qf, kvs_bf, kvw_bf, kvf_bf = map(
        r3, (kvc, kvs, kvw, kvf, small, zn, zf, qn, qf, kvs_bf, kvw_bf, kvf_bf))

    ck_t = _logf_cumsum(caches['logf_t'], page_table, _token_minor(small[..., :H_FOX], ts))
    cq = jnp.swapaxes(ck_t[:, n_pages, :, :ts], 1, 2).reshape(b, ts * H_FOX, 1)
    eye = jnp.eye(H_FOX, dtype=BF16)
    qbd = (qf[:, :ts].reshape(b, ts, H_FOX, 1, HEAD_DIM) * eye[None, None, :, :, None]).reshape(b, ts * H_FOX, D_FOX)
    o_f = _fox_sample(caches['fox_t'], page_table, _token_minor(kvf_bf, ts), ck_t, qbd,
                      jnp.broadcast_to(cq, (b, ts * H_FOX, LANES)), ts)

    nsub = past_len // CMP_STRIDE
    nb = -(-(past_len // SEL_BLOCK + 2) // LANES) * LANES
    _, _, tc = _prompt_tables(t5_table, nsub)
    cflat = _paged_rows(caches['cmp'], page_table)
    o_c, msk = _compressed_branch(qn, cflat.reshape(b * nsub, 4096), b, nsub, wts, tc,
                                  _importance_matrix(nb, nsub), past_len // TQ)

    def rows_gh(a):
        a = a.reshape(b, ts, GRP, KV_NSA, HEAD_DIM).transpose(0, 1, 3, 2, 4)
        z = jnp.zeros_like(a[:, :, 0])
        return jnp.stack([jnp.concatenate([a[:, :, 0], z], -1), jnp.concatenate([z, a[:, :, 1]], -1)],
                         axis=2).reshape(b, ts * 8, LANES)

    msk32 = jnp.broadcast_to(msk[:, 0, :, :ts].transpose(0, 2, 1, 3)[:, :, :, None, :],
                             (b, ts, KV_NSA, GRP, nb)).reshape(b, ts * 8, nb)
    gate32 = jnp.pad(small[:, :ts, 8:32].reshape(b, ts * 8, 3), ((0, 0), (0, 0), (0, LANES - 3)))
    bias_s, bias_w = _sample_tables(t5_table, past_len, ts, caches['win_t'].shape[-1], n_keys)
    on32 = _nsa_sample(caches['slc_t'], page_table, _token_minor(kvs_bf, ts), rows_gh(qn[:, :ts]), msk32, bias_s,
                       caches['win_t'], _token_minor(kvw_bf, ts), bias_w, rows_gh(o_c[:, :ts]), gate32)
    on5 = on32.reshape(b, ts, KV_NSA, GRP, LANES)
    o_n = jnp.stack([on5[:, :, 0, :, :HEAD_DIM], on5[:, :, 1, :, HEAD_DIM:]], axis=3).reshape(b, ts, D_NSA)

    n8 = SUBLANES
    y = _mix_out(pad_rows(x3d, n8).reshape(b * n8, D_MODEL), pad_rows(o_n, n8).reshape(b * n8, 512),
                 zn[:, :n8].reshape(b * n8, 512), o_f.reshape(b * n8, 512), zf[:, :n8].reshape(b * n8, 512),
                 pad_rows(p3d, n8).reshape(b * n8, PLE_DIM), wts['wo'], wts['g_post'], wts['wpp'], wts['wpg'])
    return (y.reshape(b, n8, D_MODEL)[:, :ts], kvc[:, :ts], kvs[:, :ts], kvf[:, :ts], small[:, :ts, :H_FOX],
            kvw[:, :ts])


def kernel(x_prompt, x_sample, cache_cmp_kv, cache_slc_kv, cache_fox_kv, cache_fox_logf, state_win_kv, page_table, p_prompt, p_sample, g_pre, g_post, w_in, b_fgate, cmp_pos, w_cmp1, w_cmp2, w_out, w_pproj, w_pgate, t5_table):
    assert g_pre.shape[0] == 1, "single-layer trunk"
    bp, t, _ = x_prompt.shape
    bs, ts, _ = x_sample.shape
    n_pool = cache_cmp_kv.shape[0]
    n_pages = page_table.shape[1]
    past_len = n_pages * PAGE_SIZE
    w_buf = state_win_kv.shape[2]
    assert t % CUM_ROWS == 0 and n_pages % PAGES_PER_STEP == 0 and w_buf == WINDOW and ts <= SUBLANES

    wts = _prep_weights(g_pre[0], g_post[0], w_in[0], b_fgate[0], cmp_pos[0], w_cmp1[0], w_cmp2[0], w_out[0],
                        w_pproj[0], w_pgate[0])
    kv5 = lambda a, n: a.reshape(a.shape[0], 1, a.shape[1], 2, n, HEAD_DIM)

    yp, kvc, kvs, kvf, logf, kvw = _prompt_group(x_prompt, p_prompt[0], wts, t5_table)
    w_keep = min(WINDOW, t)
    outs_p = (kv5(kvc, KV_NSA), kv5(kvs, KV_NSA), kv5(kvf, H_FOX), logf[:, None], kv5(kvw[:, t - w_keep:], KV_NSA))

    feat_major = lambda a: jnp.moveaxis(a[:, 0], 1, -1).reshape(a.shape[0], -1, a.shape[2])
    caches = dict(cmp=cache_cmp_kv[:, 0].reshape(n_pool, PAGE_SIZE // CMP_STRIDE, 4096),
                  slc_t=feat_major(cache_slc_kv), fox_t=feat_major(cache_fox_kv),
                  logf_t=feat_major(cache_fox_logf), win_t=feat_major(state_win_kv))
    ys, kvc_s, kvs_s, kvf_s, logf_s, kvw_s = _sample_group(x_sample, p_sample[0], wts, t5_table, caches, page_table)
    win_new = jnp.concatenate([state_win_kv[:, 0].reshape(bs, w_buf, 256), kvw_s], axis=1)
    w_keep_s = min(WINDOW, past_len + ts)
    outs_s = (kv5(kvc_s, KV_NSA), kv5(kvs_s, KV_NSA), kv5(kvf_s, H_FOX), logf_s[:, None],
              kv5(win_new[:, win_new.shape[1] - w_keep_s:], KV_NSA))
    return (yp, ys) + outs_p + outs_s
```

```python
import functools
import math

import numpy as np
import jax
import jax.numpy as jnp
from jax import lax
from jax.experimental import pallas as pl
from jax.experimental.pallas import tpu as pltpu

F32 = jnp.float32
BF16 = jnp.bfloat16

D_MODEL = 1024
HEAD_DIM = 64
KV_NSA = 2
GRP = 4
H_FOX = 8
D_NSA = 512
D_FOX = 512
CMP_STRIDE = 16
CMP_LEN = 32
CMP_HIDDEN = 128
SEL_BLOCK = 64
TOP_N = 16
WINDOW = 512
T5_BUCKETS = 32
T5_EXACT = 16
T5_MAX_DIST = 1024
PLE_DIM = 256
PAGE_SIZE = 128
RMS_EPS = 1e-6
NEG_INF = -1e30
FORCED_SCORE = 1e9

LANES = 128
SUBLANES = 8
TQ = 128
BK = 512
CUM_ROWS = 1024
FOX_PAGES = 16
NSA_PAGES = 16
LOGF_PAGES = 32
CMP_PAGES = 32
VMEM_LIMIT = 56 * 1024 * 1024
N_PROJ = 3968
BIAS_SPAN = 1536

_NT = (((1,), (1,)), ((), ()))


def _cparams(sem):
    return pltpu.CompilerParams(dimension_semantics=sem, vmem_limit_bytes=VMEM_LIMIT)


def _sigmoid(x):
    return 1.0 / (1.0 + jnp.exp(-x))


def _split3(x):
    a = x.astype(BF16)
    r = x - a.astype(F32)
    b = r.astype(BF16)
    c = (r - b.astype(F32)).astype(BF16)
    return a, b, c


def _dot3(lhs_parts, rhs):
    a, b, c = (jnp.dot(p, rhs, preferred_element_type=F32) for p in lhs_parts)
    return a + b + c


def _dot3_rhs(lhs, rhs_parts):
    a, b, c = (jnp.dot(lhs, p, preferred_element_type=F32) for p in rhs_parts)
    return a + b + c


def _proj_kernel(x_ref, g_ref, w_ref, bf_ref, kvc_ref, small_ref, zn_ref, zf_ref, qn_ref, qf_ref, kvsb_ref,
                 kvwb_ref, kvfb_ref, *kv_refs, feature_major):
    x = x_ref[...]
    h = x * lax.rsqrt(jnp.mean(x * x, axis=-1, keepdims=True) + RMS_EPS) * g_ref[...]
    u = jnp.dot(h.astype(BF16), w_ref[...], preferred_element_type=F32)
    scale = HEAD_DIM ** -0.5
    qn_ref[...] = (u[:, 0:512] * scale).astype(BF16)
    kvc = u[:, 512:768]
    kvc_ref[...] = kvc
    kvs = u[:, 768:1024]
    kvsb_ref[...] = kvs.astype(BF16)
    kvw = u[:, 1024:1280]
    kvwb_ref[...] = kvw.astype(BF16)
    zn_ref[...] = u[:, 1280:1792]
    qf_ref[...] = (u[:, 1792:2304] * scale).astype(BF16)
    kvf = u[:, 2304:3328]
    kvfb_ref[...] = kvf.astype(BF16)
    zf_ref[...] = u[:, 3328:3840]
    s = u[:, 3840:3968]
    lane = lax.broadcasted_iota(jnp.int32, s.shape, 1)
    f = s + bf_ref[...]
    logf = jnp.minimum(f, 0.0) - jnp.log(1.0 + jnp.exp(-jnp.abs(f)))
    small = jnp.where(lane < 8, logf, jnp.where(lane < 32, _sigmoid(s), 0.0))
    small_ref[...] = small
    if feature_major:
        kvct_ref, kvst_ref, kvwt_ref, kvft_ref, logft_ref = kv_refs
        kvct_ref[...] = kvc.T
        kvst_ref[...] = kvs.T
        kvwt_ref[...] = kvw.T
        kvft_ref[...] = kvf.T
        logft_ref[...] = small.T[0:H_FOX, :]
    else:
        kvs_ref, kvw_ref, kvf_ref = kv_refs
        kvs_ref[...] = kvs
        kvw_ref[...] = kvw
        kvf_ref[...] = kvf


def _proj(x3d, g_pre, w_perm, bf_pad, feature_major, bm=256):
    b, t, _ = x3d.shape
    r = b * t
    bm = min(bm, t)
    nt = t // bm
    row = lambda c: pl.BlockSpec((bm, c), lambda i: (i, 0))
    col = lambda c: pl.BlockSpec((None, c, bm), lambda i: (i // nt, 0, i % nt))
    const = lambda a, b: pl.BlockSpec((a, b), lambda i: (0, 0))
    f32 = lambda c: jax.ShapeDtypeStruct((r, c), F32)
    b16 = lambda c: jax.ShapeDtypeStruct((r, c), BF16)
    f32t = lambda c: jax.ShapeDtypeStruct((b, c, t), F32)
    if feature_major:
        kv_specs = [col(256), col(256), col(256), col(1024), col(H_FOX)]
        kv_shapes = [f32t(256), f32t(256), f32t(256), f32t(1024), f32t(H_FOX)]
    else:
        kv_specs = [row(256), row(256), row(1024)]
        kv_shapes = [f32(256), f32(256), f32(1024)]
    return pl.pallas_call(
        functools.partial(_proj_kernel, feature_major=feature_major),
        grid=(r // bm,),
        in_specs=[row(D_MODEL), const(1, D_MODEL), const(D_MODEL, N_PROJ), const(1, LANES)],
        out_specs=[row(256), row(LANES), row(512), row(512), row(512), row(512), row(256), row(256),
                   row(1024)] + kv_specs,
        out_shape=[f32(256), f32(LANES), f32(512), f32(512), b16(512), b16(512), b16(256), b16(256),
                   b16(1024)] + kv_shapes,
        compiler_params=_cparams(("parallel",)),
    )(x3d.reshape(r, D_MODEL), g_pre, w_perm, bf_pad)


def _cumsum_kernel(x_ref, cum_ref, cumt_ref, carry_ref):
    @pl.when(pl.program_id(1) == 0)
    def _():
        carry_ref[...] = jnp.zeros_like(carry_ref)

    rows = lax.broadcasted_iota(jnp.int32, (LANES, LANES), 0)
    cols = lax.broadcasted_iota(jnp.int32, (LANES, LANES), 1)
    tri = jnp.where(rows >= cols, 1.0, 0.0).astype(BF16)
    carry = carry_ref[0:1, :]
    for s in range(CUM_ROWS // LANES):
        sl = slice(LANES * s, LANES * (s + 1))
        y = _dot3_rhs(tri, _split3(x_ref[sl, :])) + carry
        cum_ref[sl, :] = y
        cumt_ref[:, sl] = y.T
        carry = y[LANES - 1:LANES, :]
    carry_ref[...] = jnp.broadcast_to(carry, carry_ref.shape)


def _cumsum(small3d):
    b, t, _ = small3d.shape
    return pl.pallas_call(
        _cumsum_kernel,
        grid=(b, t // CUM_ROWS),
        in_specs=[pl.BlockSpec((None, CUM_ROWS, LANES), lambda bi, j: (bi, j, 0))],
        out_specs=[pl.BlockSpec((None, CUM_ROWS, LANES), lambda bi, j: (bi, j, 0)),
                   pl.BlockSpec((None, LANES, CUM_ROWS), lambda bi, j: (bi, 0, j))],
        out_shape=[jax.ShapeDtypeStruct((b, t, LANES), F32), jax.ShapeDtypeStruct((b, LANES, t), F32)],
        scratch_shapes=[pltpu.VMEM((SUBLANES, LANES), F32)],
        compiler_params=_cparams(("parallel", "arbitrary")),
    )(small3d)


def _fox_kernel(q_ref, k_ref, v_ref, cum_ref, cumt_ref, o_ref):
    pp = pl.program_id(1)
    q0 = pl.program_id(2) * TQ
    lane = lax.broadcasted_iota(jnp.int32, (TQ, LANES), 1)
    lo = lane < 64
    cum = cum_ref[...]
    qs, cqs = [], []
    for a in range(2):
        qt = q_ref[:, LANES * a:LANES * (a + 1)]
        zero = jnp.zeros_like(qt)
        for hh in range(2):
            qs.append(jnp.where(lo, qt, zero) if hh == 0 else jnp.where(lo, zero, qt))
            head = 4 * pp + 2 * a + hh
            cqs.append(jnp.sum(jnp.where(lane == head, cum, 0.0), axis=1, keepdims=True))
    diff = (lax.broadcasted_iota(jnp.int32, (TQ, BK), 1) - lax.broadcasted_iota(jnp.int32, (TQ, BK), 0))
    n_full = (q0 + 1) // BK

    def tile(j, carry, masked):
        k0 = pl.multiple_of(j * BK, BK)
        kts = [k_ref[pl.ds(k0, BK), LANES * a:LANES * (a + 1)] for a in range(2)]
        vts = [v_ref[pl.ds(k0, BK), LANES * a:LANES * (a + 1)] for a in range(2)]
        ss = []
        for n in range(4):
            ck = cumt_ref[pl.ds(4 * pp + n, 1), pl.ds(k0, BK)]
            s = lax.dot_general(qs[n], kts[n // 2], _NT, preferred_element_type=F32) + (cqs[n] - ck)
            ss.append(jnp.where(diff <= (q0 - k0), s, NEG_INF) if masked else s)
        stats, prs = [], []
        for n in range(4):
            m_prev, l_prev, _ = carry[n]
            m_new = jnp.maximum(m_prev, jnp.max(ss[n], axis=1, keepdims=True))
            alpha = jnp.exp(m_prev - m_new)
            pr = jnp.exp(ss[n] - jnp.tile(m_new, (1, BK // LANES)))
            stats.append((m_new, alpha * l_prev + jnp.sum(pr, axis=1, keepdims=True), alpha))
            prs.append(pr.astype(BF16))
        return tuple((stats[n][0], stats[n][1],
                      stats[n][2] * carry[n][2] + jnp.dot(prs[n], vts[n // 2], preferred_element_type=F32))
                     for n in range(4))

    init = tuple((jnp.full((TQ, LANES), NEG_INF, F32), jnp.zeros((TQ, LANES), F32), jnp.zeros((TQ, LANES), F32))
                 for _ in range(4))
    carry = lax.fori_loop(0, n_full, lambda j, c: tile(j, c, False), init)
    res = tile(n_full, carry, True)
    for a in range(2):
        o_ref[:, LANES * a:LANES * (a + 1)] = jnp.where(lo, res[2 * a][2] / res[2 * a][1],
                                                        res[2 * a + 1][2] / res[2 * a + 1][1])


def _fox(qf, kvf_bf, cum, cumt):
    b, t, _ = qf.shape
    return pl.pallas_call(
        _fox_kernel,
        grid=(b, 2, t // TQ),
        in_specs=[pl.BlockSpec((None, TQ, 256), lambda bi, p, i: (bi, i, p)),
                  pl.BlockSpec((None, t, 256), lambda bi, p, i: (bi, 0, p)),
                  pl.BlockSpec((None, t, 256), lambda bi, p, i: (bi, 0, 2 + p)),
                  pl.BlockSpec((None, TQ, LANES), lambda bi, p, i: (bi, i, 0)),
                  pl.BlockSpec((None, SUBLANES, t), lambda bi, p, i: (bi, 0, 0))],
        out_specs=pl.BlockSpec((None, TQ, 256), lambda bi, p, i: (bi, i, p)),
        out_shape=jax.ShapeDtypeStruct((b, t, D_FOX), F32),
        compiler_params=_cparams(("parallel", "parallel", "arbitrary")),
    )(qf, kvf_bf, kvf_bf, cum, cumt)


def _cmp1_kernel(x_ref, pos_ref, w0_ref, w1_ref, p_ref):
    x = x_ref[...]
    x0 = (x + pos_ref[0:1, :]).astype(BF16)
    x1 = (x + pos_ref[1:2, :]).astype(BF16)
    p_ref[:, 0:512] = jnp.dot(x0, w0_ref[...], preferred_element_type=F32)
    p_ref[:, 512:1024] = jnp.dot(x1, w1_ref[...], preferred_element_type=F32)


def _cmp1(xflat, pos2, w0, w1, bm=256):
    r = xflat.shape[0]
    bm = min(bm, r)
    return pl.pallas_call(
        _cmp1_kernel,
        grid=(r // bm,),
        in_specs=[pl.BlockSpec((bm, 4096), lambda i: (i, 0)), pl.BlockSpec((2, 4096), lambda i: (0, 0)),
                  pl.BlockSpec((4096, 512), lambda i: (0, 0)), pl.BlockSpec((4096, 512), lambda i: (0, 0))],
        out_specs=pl.BlockSpec((bm, 1024), lambda i: (i, 0)),
        out_shape=jax.ShapeDtypeStruct((r, 1024), F32),
        compiler_params=_cparams(("parallel",)),
    )(xflat, pos2, w0, w1)


def _cmp2_kernel(p_ref, w2k_ref, w2v_ref, ck_ref, cvt_ref, *, nsub):
    pre = p_ref[:, 0:512] + pltpu.roll(p_ref[:, 512:1024], nsub - 1, 0)
    hid = (pre * _sigmoid(pre)).astype(BF16)
    ck_ref[...] = jnp.dot(hid[:, 0:256], w2k_ref[...], preferred_element_type=F32).astype(BF16)
    cv = jnp.dot(hid[:, 256:512], w2v_ref[...], preferred_element_type=F32)
    cvt_ref[...] = cv.T.astype(BF16)


def _cmp2(p3d, w2k, w2v):
    b, nsub, _ = p3d.shape
    return pl.pallas_call(
        functools.partial(_cmp2_kernel, nsub=nsub),
        grid=(b,),
        in_specs=[pl.BlockSpec((None, nsub, 1024), lambda bi: (bi, 0, 0)),
                  pl.BlockSpec((256, LANES), lambda bi: (0, 0)), pl.BlockSpec((256, LANES), lambda bi: (0, 0))],
        out_specs=[pl.BlockSpec((None, nsub, LANES), lambda bi: (bi, 0, 0)),
                   pl.BlockSpec((None, LANES, nsub), lambda bi: (bi, 0, 0))],
        out_shape=[jax.ShapeDtypeStruct((b, nsub, LANES), BF16), jax.ShapeDtypeStruct((b, LANES, nsub), BF16)],
        compiler_params=_cparams(("parallel",)),
    )(p3d, w2k, w2v)


def _nsa_cmp_kernel(q_ref, ck_ref, cvt_ref, tc_ref, a_ref, oc_ref, msk_ref, *, i_off, nsub, nb):
    ig = pl.program_id(1) + i_off
    n_cmp = nsub - 1
    n_io = lax.broadcasted_iota(jnp.int32, (nsub, TQ), 0)
    t_io = lax.broadcasted_iota(jnp.int32, (nsub, TQ), 1)
    valid = ((TQ * ig + t_io - CMP_STRIDE * n_io - (CMP_LEN - 1)) >= 0) & (n_io < n_cmp)
    start = pl.multiple_of(nsub - 8 * ig, 8)
    lane = lax.broadcasted_iota(jnp.int32, (TQ, LANES), 1)
    sub_lo = lax.broadcasted_iota(jnp.int32, (LANES, TQ), 0) < 64
    ck = ck_ref[...]
    cvt = cvt_ref[...]
    imps = [jnp.zeros((nsub, TQ), F32), jnp.zeros((nsub, TQ), F32)]
    for h in range(GRP):
        qt = q_ref[:, LANES * h:LANES * (h + 1)]
        zero = jnp.zeros_like(qt)
        halves = []
        for g in range(KV_NSA):
            qm = jnp.where(lane < 64, qt, zero) if g == 0 else jnp.where(lane < 64, zero, qt)
            st = lax.dot_general(ck, qm, _NT, preferred_element_type=F32)
            st = st + tc_ref[4 * g + h, pl.ds(start, nsub), :]
            l = jnp.where(valid, st, NEG_INF)
            m = jnp.max(l, axis=0, keepdims=True)
            e = jnp.where(valid, jnp.exp(l - m), 0.0)
            pc = e / jnp.maximum(jnp.sum(e, axis=0, keepdims=True), 1e-30)
            imps[g] = imps[g] + pc
            halves.append(jnp.dot(cvt, pc.astype(BF16), preferred_element_type=F32))
        oc_ref[:, LANES * h:LANES * (h + 1)] = jnp.where(sub_lo, halves[0], halves[1]).T

    blk = lax.broadcasted_iota(jnp.int32, (nb, TQ), 0)
    qpos = TQ * ig + lax.broadcasted_iota(jnp.int32, (nb, TQ), 1)
    cur = qpos >> 6
    forced = (blk == 0) | (blk == cur) | (blk == cur - 1)
    future = blk * SEL_BLOCK > qpos
    amat = a_ref[...]
    for g in range(KV_NSA):
        a, b, c = _split3(imps[g])
        imp_sel = (jnp.dot(amat, a, preferred_element_type=F32) + jnp.dot(amat, b, preferred_element_type=F32)
                   + jnp.dot(amat, c, preferred_element_type=F32))
        score = jnp.where(future, NEG_INF, jnp.where(forced, FORCED_SCORE, imp_sel))

        def pick(_, carry):
            sc, sel = carry
            mx = jnp.max(sc, axis=0, keepdims=True)
            idx = jnp.min(jnp.where(sc == mx, blk, nb), axis=0, keepdims=True)
            hit = blk == idx
            return jnp.where(hit, -jnp.inf, sc), jnp.where(hit, 1.0, sel)

        _, sel = lax.fori_loop(0, TOP_N, pick, (score, jnp.zeros((nb, TQ), F32)))
        msk_ref[g] = sel.T.astype(BF16)


def _nsa_cmp(qn, ck, cvt, tc, amat, i_off):
    b, tq_all, _ = qn.shape
    nsub = ck.shape[1]
    nb = amat.shape[0]
    nq = tq_all // TQ
    return pl.pallas_call(
        functools.partial(_nsa_cmp_kernel, i_off=i_off, nsub=nsub, nb=nb),
        grid=(b, nq),
        in_specs=[pl.BlockSpec((None, TQ, D_NSA), lambda bi, i: (bi, i, 0)),
                  pl.BlockSpec((None, nsub, LANES), lambda bi, i: (bi, 0, 0)),
                  pl.BlockSpec((None, LANES, nsub), lambda bi, i: (bi, 0, 0)),
                  pl.BlockSpec((8, 2 * nsub, LANES), lambda bi, i: (0, 0, 0)),
                  pl.BlockSpec((nb, nsub), lambda bi, i: (0, 0))],
        out_specs=[pl.BlockSpec((None, TQ, D_NSA), lambda bi, i: (bi, i, 0)),
                   pl.BlockSpec((None, None, 2, TQ, nb), lambda bi, i: (bi, i, 0, 0, 0))],
        out_shape=[jax.ShapeDtypeStruct((b, tq_all, D_NSA), F32),
                   jax.ShapeDtypeStruct((b, nq, 2, TQ, nb), BF16)],
        compiler_params=_cparams(("parallel", "arbitrary")),
    )(qn, ck, cvt, tc, amat)


def _nsa_sw_kernel(q_ref, ks_ref, kw_ref, msk_ref, bts_ref, btw_ref, oc_ref, sm_ref, on_ref, *, nb):
    ig = pl.program_id(1)
    lane = lax.broadcasted_iota(jnp.int32, (TQ, LANES), 1)
    lo = lane < 64
    blk_row = lax.broadcasted_iota(jnp.int32, (nb, BK), 0)
    key_blk = lax.broadcasted_iota(jnp.int32, (nb, BK), 1) >> 6
    n_tiles = (TQ * ig + TQ - 1) // BK + 1
    sub = BK // LANES
    o_sel = {}
    o_win = {}
    for g in range(KV_NSA):
        qms = []
        for h in range(GRP):
            qt = q_ref[:, LANES * h:LANES * (h + 1)]
            zero = jnp.zeros_like(qt)
            qms.append(jnp.where(lo, qt, zero) if g == 0 else jnp.where(lo, zero, qt))
        mg = msk_ref[g]

        def body(j, carry):
            k0 = pl.multiple_of(j * BK, BK)
            kt = ks_ref[pl.ds(k0, BK), 0:LANES]
            vt = ks_ref[pl.ds(k0, BK), LANES:2 * LANES]
            expand = jnp.where(blk_row == key_blk + (BK // SEL_BLOCK) * j, 1.0, 0.0).astype(BF16)
            picked = jnp.dot(mg, expand, preferred_element_type=F32) > 0.5
            es = [jnp.clip(ig - (sub * j + c) + 1, 0, 9) for c in range(sub)]
            ss = []
            for h in range(GRP):
                s = lax.dot_general(qms[h], kt, _NT, preferred_element_type=F32)
                parts = [s[:, LANES * c:LANES * (c + 1)] + bts_ref[4 * g + h, es[c]] for c in range(sub)]
                ss.append(jnp.where(picked, jnp.concatenate(parts, axis=1), NEG_INF))
            stats, prs = [], []
            for h in range(GRP):
                m_prev, l_prev, _ = carry[h]
                m_new = jnp.maximum(m_prev, jnp.max(ss[h], axis=1, keepdims=True))
                alpha = jnp.exp(m_prev - m_new)
                pr = jnp.exp(ss[h] - jnp.tile(m_new, (1, sub)))
                stats.append((m_new, alpha * l_prev + jnp.sum(pr, axis=1, keepdims=True), alpha))
                prs.append(pr.astype(BF16))
            return tuple((stats[h][0], stats[h][1],
                          stats[h][2] * carry[h][2] + jnp.dot(prs[h], vt, preferred_element_type=F32))
                         for h in range(GRP))

        init = tuple((jnp.full((TQ, LANES), NEG_INF, F32), jnp.zeros((TQ, LANES), F32),
                      jnp.zeros((TQ, LANES), F32)) for _ in range(GRP))
        res = lax.fori_loop(0, n_tiles, body, init)
        for h in range(GRP):
            o_sel[(g, h)] = res[h][2] / res[h][1]

        n_w = WINDOW // TQ + 1
        kts, vts, es = [], [], []
        for d in range(n_w):
            tw = ig - d
            r0 = pl.multiple_of(jnp.maximum(tw, 0) * TQ, TQ)
            kts.append(kw_ref[pl.ds(r0, TQ), 0:LANES])
            vts.append(kw_ref[pl.ds(r0, TQ), LANES:2 * LANES])
            es.append(jnp.where(tw >= 0, d, n_w))
        for h in range(GRP):
            ss = [lax.dot_general(qms[h], kts[d], _NT, preferred_element_type=F32) + btw_ref[4 * g + h, es[d]]
                  for d in range(n_w)]
            m = ss[0]
            for d in range(1, n_w):
                m = jnp.maximum(m, ss[d])
            m = jnp.max(m, axis=1, keepdims=True)
            l = jnp.zeros((TQ, 1), F32)
            acc = jnp.zeros((TQ, LANES), F32)
            for d in range(n_w):
                pr = jnp.exp(ss[d] - m)
                l = l + jnp.sum(pr, axis=1, keepdims=True)
                acc = acc + jnp.dot(pr.astype(BF16), vts[d], preferred_element_type=F32)
            o_win[(g, h)] = acc / l

    sm = sm_ref[...]
    for h in range(GRP):
        def gate(br):
            c0 = 8 + 3 * h + br
            c1 = 8 + 3 * GRP + 3 * h + br
            return jnp.where(lo, sm[:, c0:c0 + 1], sm[:, c1:c1 + 1])
        o_c = oc_ref[:, LANES * h:LANES * (h + 1)]
        o_s = jnp.where(lo, o_sel[(0, h)], o_sel[(1, h)])
        o_w = jnp.where(lo, o_win[(0, h)], o_win[(1, h)])
        on_ref[:, LANES * h:LANES * (h + 1)] = gate(0) * o_c + gate(1) * o_s + gate(2) * o_w


def _nsa_sw(qn, ks_bf, kw_bf, msk, bts, btw, oc, small):
    b, t, _ = qn.shape
    nb = msk.shape[-1]
    return pl.pallas_call(
        functools.partial(_nsa_sw_kernel, nb=nb),
        grid=(b, t // TQ),
        in_specs=[pl.BlockSpec((None, TQ, D_NSA), lambda bi, i: (bi, i, 0)),
                  pl.BlockSpec((None, t, 256), lambda bi, i: (bi, 0, 0)),
                  pl.BlockSpec((None, t, 256), lambda bi, i: (bi, 0, 0)),
                  pl.BlockSpec((None, None, 2, TQ, nb), lambda bi, i: (bi, i, 0, 0, 0)),
                  pl.BlockSpec((8, 10, TQ, LANES), lambda bi, i: (0, 0, 0, 0)),
                  pl.BlockSpec((8, 6, TQ, LANES), lambda bi, i: (0, 0, 0, 0)),
                  pl.BlockSpec((None, TQ, D_NSA), lambda bi, i: (bi, i, 0)),
                  pl.BlockSpec((None, TQ, LANES), lambda bi, i: (bi, i, 0))],
        out_specs=pl.BlockSpec((None, TQ, D_NSA), lambda bi, i: (bi, i, 0)),
        out_shape=jax.ShapeDtypeStruct((b, t, D_NSA), F32),
        compiler_params=_cparams(("parallel", "arbitrary")),
    )(qn, ks_bf, kw_bf, msk, bts, btw, oc, small)


def _out_kernel(x_ref, on_ref, zn_ref, of_ref, zf_ref, p_ref, wo_ref, gp_ref, wpp_ref, wpg_ref, y_ref):
    zn = zn_ref[...]
    zf = zf_ref[...]
    mix_n = (on_ref[...] * (zn * _sigmoid(zn))).astype(BF16)
    mix_f = (of_ref[...] * (zf * _sigmoid(zf))).astype(BF16)
    o = (jnp.dot(mix_n, wo_ref[0:512, :], preferred_element_type=F32)
         + jnp.dot(mix_f, wo_ref[512:1024, :], preferred_element_type=F32))
    r = o * lax.rsqrt(jnp.mean(o * o, axis=-1, keepdims=True) + RMS_EPS) * gp_ref[...]
    x1 = x_ref[...] + r
    ple = jnp.dot(p_ref[...].astype(BF16), wpp_ref[...], preferred_element_type=F32)
    gate = _sigmoid(jnp.dot(x1.astype(BF16), wpg_ref[...], preferred_element_type=F32))
    y_ref[...] = x1 + ple * gate


def _mix_out(x2d, on, zn, of, zf, p2d, wo, g_post, wpp, wpg, bm=256):
    r = x2d.shape[0]
    bm = min(bm, r)
    row = lambda c: pl.BlockSpec((bm, c), lambda i: (i, 0))
    const = lambda a, b: pl.BlockSpec((a, b), lambda i: (0, 0))
    return pl.pallas_call(
        _out_kernel,
        grid=(r // bm,),
        in_specs=[row(D_MODEL), row(512), row(512), row(512), row(512), row(PLE_DIM),
                  const(D_MODEL, D_MODEL), const(1, D_MODEL), const(PLE_DIM, D_MODEL), const(D_MODEL, D_MODEL)],
        out_specs=row(D_MODEL),
        out_shape=jax.ShapeDtypeStruct((r, D_MODEL), F32),
        compiler_params=_cparams(("parallel",)),
    )(x2d, on, zn, of, zf, p2d, wo, g_post, wpp, wpg)


def _pages_per_step(n_pages, wanted):
    pages = min(wanted, n_pages)
    assert n_pages % pages == 0
    return pages, n_pages // pages


def _page_specs(block_rows, block_cols, n_groups, pages):
    def page_map(k):
        return lambda bi, j, pt: (pt[bi, jnp.minimum(j, n_groups - 1) * pages + k], 0, 0)
    return [pl.BlockSpec((None, block_rows, block_cols), page_map(k)) for k in range(pages)]


def _cmp1_paged_kernel(pt_ref, *refs, pages):
    del pt_ref
    pos_ref, w0_ref, w1_ref, p_ref, xk_ref, xv_ref = refs[pages:]
    for i, pg in enumerate(refs[:pages]):
        xt = pg[...].T
        xk_ref[PAGE_SIZE * i:PAGE_SIZE * (i + 1), :] = xt[:, 0:LANES]
        xv_ref[PAGE_SIZE * i:PAGE_SIZE * (i + 1), :] = xt[:, LANES:2 * LANES]
    m = pages * PAGE_SIZE // CMP_STRIDE
    acc0 = jnp.zeros((m, 512), F32)
    acc1 = jnp.zeros((m, 512), F32)
    for s in range(CMP_STRIDE):
        x = jnp.concatenate([xk_ref[pl.ds(s, m, stride=CMP_STRIDE), :],
                             xv_ref[pl.ds(s, m, stride=CMP_STRIDE), :]], axis=1)
        sl = slice(256 * s, 256 * (s + 1))
        acc0 = acc0 + jnp.dot((x + pos_ref[0:1, sl]).astype(BF16), w0_ref[sl, :], preferred_element_type=F32)
        acc1 = acc1 + jnp.dot((x + pos_ref[1:2, sl]).astype(BF16), w1_ref[sl, :], preferred_element_type=F32)
    p_ref[:, 0:512] = acc0
    p_ref[:, 512:1024] = acc1


def _cmp1_paged(cmp_t, page_table, pos2, w0, w1):
    b, n_pages = page_table.shape
    pages, n_groups = _pages_per_step(n_pages, CMP_PAGES)
    m = pages * PAGE_SIZE // CMP_STRIDE
    const = lambda r, c: pl.BlockSpec((r, c), lambda bi, j, pt: (0, 0))
    return pl.pallas_call(
        functools.partial(_cmp1_paged_kernel, pages=pages),
        grid_spec=pltpu.PrefetchScalarGridSpec(
            num_scalar_prefetch=1, grid=(b, n_groups),
            in_specs=_page_specs(256, PAGE_SIZE, n_groups, pages) + [const(2, 4096), const(4096, 512),
                                                                     const(4096, 512)],
            out_specs=pl.BlockSpec((None, m, 1024), lambda bi, j, pt: (bi, j, 0)),
            scratch_shapes=[pltpu.VMEM((pages * PAGE_SIZE, LANES), F32)] * 2),
        out_shape=jax.ShapeDtypeStruct((b, n_groups * m, 1024), F32),
        compiler_params=_cparams(("parallel", "arbitrary")),
    )(page_table, *([cmp_t] * pages), pos2, w0, w1)


def _logf_cumsum_kernel(pt_ref, *refs, n_groups, pages):
    del pt_ref
    new_ref, out_ref, xs_ref, carry_ref = refs[pages:]
    pages = refs[:pages]
    j = pl.program_id(1)

    @pl.when(j == 0)
    def _():
        carry_ref[...] = jnp.zeros_like(carry_ref)

    @pl.when(j < n_groups)
    def _():
        for i, pg in enumerate(pages):
            xs_ref[SUBLANES * i:SUBLANES * (i + 1), :] = pg[...]

    @pl.when(j == n_groups)
    def _():
        xs_ref[...] = jnp.zeros_like(xs_ref)
        xs_ref[0:SUBLANES, :] = new_ref[...]

    rows = lax.broadcasted_iota(jnp.int32, (LANES, LANES), 0)
    cols = lax.broadcasted_iota(jnp.int32, (LANES, LANES), 1)
    triu = jnp.where(rows <= cols, 1.0, 0.0).astype(BF16)
    y = _dot3(_split3(xs_ref[...]), triu)
    carry = carry_ref[...]
    for i in range(len(pages)):
        yi = y[SUBLANES * i:SUBLANES * (i + 1), :] + carry
        out_ref[i] = yi
        carry = jnp.broadcast_to(yi[:, LANES - 1:LANES], (SUBLANES, LANES))
    carry_ref[...] = carry


def _logf_cumsum(logf_t, page_table, new_t):
    b, n_pages = page_table.shape
    pages, n_groups = _pages_per_step(n_pages, LOGF_PAGES)
    in_specs = _page_specs(H_FOX, PAGE_SIZE, n_groups, pages)
    in_specs.append(pl.BlockSpec((None, H_FOX, LANES), lambda bi, j, pt: (bi, 0, 0)))
    return pl.pallas_call(
        functools.partial(_logf_cumsum_kernel, n_groups=n_groups, pages=pages),
        grid_spec=pltpu.PrefetchScalarGridSpec(
            num_scalar_prefetch=1, grid=(b, n_groups + 1), in_specs=in_specs,
            out_specs=pl.BlockSpec((None, pages, H_FOX, LANES), lambda bi, j, pt: (bi, j, 0, 0)),
            scratch_shapes=[pltpu.VMEM((pages * SUBLANES, LANES), F32),
                            pltpu.VMEM((SUBLANES, LANES), F32)]),
        out_shape=jax.ShapeDtypeStruct((b, (n_groups + 1) * pages, H_FOX, LANES), F32),
        compiler_params=_cparams(("parallel", "arbitrary")),
    )(page_table, *([logf_t] * pages), new_t)


def _fill_keys(j, n_groups, pages, new_ref, kv_scr):
    @pl.when(j < n_groups)
    def _():
        for i, pg in enumerate(pages):
            kv_scr[:, LANES * i:LANES * (i + 1)] = pg[...].astype(BF16)

    @pl.when(j == n_groups)
    def _():
        kv_scr[...] = jnp.zeros_like(kv_scr)
        kv_scr[:, 0:LANES] = new_ref[...]


def _online_update(s, m_scr, l_scr, acc_scr, v_t):
    nk = s.shape[1]
    m_prev = m_scr[...]
    m_new = jnp.maximum(m_prev, jnp.max(s, axis=1, keepdims=True))
    alpha = jnp.exp(m_prev - m_new)
    pr = jnp.exp(s - jnp.tile(m_new, (1, nk // LANES)))
    l_scr[...] = alpha * l_scr[...] + jnp.sum(pr, axis=1, keepdims=True)
    pv = lax.dot_general(pr.astype(BF16), v_t, _NT, preferred_element_type=F32)
    acc_scr[...] = jnp.tile(alpha, (1, acc_scr.shape[1] // LANES)) * acc_scr[...] + pv
    m_scr[...] = m_new


def _fox_sample_kernel(pt_ref, *refs, n_groups, pages, ts):
    del pt_ref
    new_ref, ck_ref, q_ref, cq_ref, o_ref, kv_scr, m_scr, l_scr, acc_scr = refs[pages:]
    n_pg, pages = pages, refs[:pages]
    j = pl.program_id(1)
    nk = n_pg * PAGE_SIZE
    nr = ts * H_FOX

    @pl.when(j == 0)
    def _():
        m_scr[...] = jnp.full_like(m_scr, NEG_INF)
        l_scr[...] = jnp.zeros_like(l_scr)
        acc_scr[...] = jnp.zeros_like(acc_scr)

    _fill_keys(j, n_groups, pages, new_ref, kv_scr)
    s = jnp.dot(q_ref[...], kv_scr[0:D_FOX, :], preferred_element_type=F32)
    ck = jnp.concatenate([jnp.tile(ck_ref[i], (ts, 1)) for i in range(n_pg)], axis=1)
    s = s + (jnp.tile(cq_ref[...], (1, n_pg)) - ck)
    lane = lax.broadcasted_iota(jnp.int32, (nr, nk), 1)
    row_tok = lax.broadcasted_iota(jnp.int32, (nr, nk), 0) >> 3
    limit = jnp.where(j < n_groups, nk, row_tok)
    s = jnp.where(lane <= limit, s, NEG_INF)
    _online_update(s, m_scr, l_scr, acc_scr, kv_scr[D_FOX:2 * D_FOX, :])

    @pl.when(j == n_groups)
    def _():
        o = acc_scr[...] / jnp.tile(l_scr[...], (1, D_FOX // LANES))
        own = ((lax.broadcasted_iota(jnp.int32, (nr, D_FOX), 0) & 7)
               == (lax.broadcasted_iota(jnp.int32, (nr, D_FOX), 1) >> 6))
        o = jnp.where(own, o, 0.0)
        rows = [jnp.sum(o[H_FOX * t:H_FOX * (t + 1)], axis=0, keepdims=True) for t in range(ts)]
        o_ref[...] = jnp.concatenate(rows + [jnp.zeros((SUBLANES - ts, D_FOX), F32)], axis=0)


def _fox_sample(fox_t, page_table, new_t, ck_t, qbd, cq, ts):
    b, n_pages = page_table.shape
    pages, n_groups = _pages_per_step(n_pages, FOX_PAGES)
    nk = pages * PAGE_SIZE
    nr = ts * H_FOX
    fixed = lambda shape: pl.BlockSpec((None,) + shape, lambda bi, j, pt: (bi,) + (0,) * len(shape))
    in_specs = _page_specs(2 * D_FOX, PAGE_SIZE, n_groups, pages)
    in_specs += [fixed((2 * D_FOX, LANES)),
                 pl.BlockSpec((None, pages, H_FOX, LANES), lambda bi, j, pt: (bi, j, 0, 0)),
                 fixed((nr, D_FOX)), fixed((nr, LANES))]
    return pl.pallas_call(
        functools.partial(_fox_sample_kernel, n_groups=n_groups, pages=pages, ts=ts),
        grid_spec=pltpu.PrefetchScalarGridSpec(
            num_scalar_prefetch=1, grid=(b, n_groups + 1), in_specs=in_specs,
            out_specs=fixed((SUBLANES, D_FOX)),
            scratch_shapes=[pltpu.VMEM((2 * D_FOX, nk), BF16), pltpu.VMEM((nr, LANES), F32),
                            pltpu.VMEM((nr, LANES), F32), pltpu.VMEM((nr, D_FOX), F32)]),
        out_shape=jax.ShapeDtypeStruct((b, SUBLANES, D_FOX), F32),
        compiler_params=_cparams(("parallel", "arbitrary")),
    )(page_table, *([fox_t] * pages), new_t, ck_t, qbd, cq)


def _nsa_sample_kernel(pt_ref, *refs, n_groups, pages, nb):
    del pt_ref
    (new_ref, q_ref, msk_ref, bias_ref, wkv_ref, wnew_ref, biasw_ref, oc_ref, gate_ref, o_ref,
     kv_scr, m_scr, l_scr, acc_scr) = refs[pages:]
    nk = pages * PAGE_SIZE
    pages = refs[:pages]
    j = pl.program_id(1)

    @pl.when(j == 0)
    def _():
        m_scr[...] = jnp.full_like(m_scr, NEG_INF)
        l_scr[...] = jnp.zeros_like(l_scr)
        acc_scr[...] = jnp.zeros_like(acc_scr)

    _fill_keys(j, n_groups, pages, new_ref, kv_scr)
    q = q_ref[...]
    s = jnp.dot(q, kv_scr[0:LANES, :], preferred_element_type=F32) + bias_ref[...]
    blk_row = lax.broadcasted_iota(jnp.int32, (nb, nk), 0)
    key_blk = lax.broadcasted_iota(jnp.int32, (nb, nk), 1) >> 6
    expand = jnp.where(blk_row == key_blk + (nk // SEL_BLOCK) * j, 1.0, 0.0).astype(BF16)
    picked = jnp.dot(msk_ref[...], expand, preferred_element_type=F32) > 0.5
    s = jnp.where(picked, s, NEG_INF)
    _online_update(s, m_scr, l_scr, acc_scr, kv_scr[LANES:2 * LANES, :])

    @pl.when(j == n_groups)
    def _():
        o_s = acc_scr[...] / l_scr[...]
        wkv = wkv_ref[...].astype(BF16)
        wnew = wnew_ref[...]
        k_w = jnp.concatenate([wkv[0:LANES], wnew[0:LANES]], axis=1)
        v_w = jnp.concatenate([wkv[LANES:2 * LANES], wnew[LANES:2 * LANES]], axis=1)
        sw = jnp.dot(q, k_w, preferred_element_type=F32) + biasw_ref[...]
        pw = jnp.exp(sw - jnp.max(sw, axis=1, keepdims=True))
        o_w = (lax.dot_general(pw.astype(BF16), v_w, _NT, preferred_element_type=F32)
               / jnp.sum(pw, axis=1, keepdims=True))
        gt = gate_ref[...]
        o_ref[...] = gt[:, 0:1] * oc_ref[...] + gt[:, 1:2] * o_s + gt[:, 2:3] * o_w


def _nsa_sample(slc_t, page_table, new_t, q32, msk32, bias_s, win_t, wnew_t, bias_w, oc32, gate32):
    b, n_pages = page_table.shape
    pages, n_groups = _pages_per_step(n_pages, NSA_PAGES)
    nk = pages * PAGE_SIZE
    nr = q32.shape[1]
    nb = msk32.shape[-1]
    w_buf = win_t.shape[-1]
    fixed = lambda shape: pl.BlockSpec((None,) + shape, lambda bi, j, pt: (bi,) + (0,) * len(shape))
    in_specs = _page_specs(256, PAGE_SIZE, n_groups, pages)
    in_specs += [fixed((256, LANES)), fixed((nr, LANES)), fixed((nr, nb)),
                 pl.BlockSpec((nr, nk), lambda bi, j, pt: (0, j)),
                 fixed((256, w_buf)), fixed((256, LANES)),
                 pl.BlockSpec((nr, w_buf + LANES), lambda bi, j, pt: (0, 0)),
                 fixed((nr, LANES)), fixed((nr, LANES))]
    return pl.pallas_call(
        functools.partial(_nsa_sample_kernel, n_groups=n_groups, pages=pages, nb=nb),
        grid_spec=pltpu.PrefetchScalarGridSpec(
            num_scalar_prefetch=1, grid=(b, n_groups + 1), in_specs=in_specs,
            out_specs=fixed((nr, LANES)),
            scratch_shapes=[pltpu.VMEM((256, nk), BF16), pltpu.VMEM((nr, LANES), F32),
                            pltpu.VMEM((nr, LANES), F32), pltpu.VMEM((nr, LANES), F32)]),
        out_shape=jax.ShapeDtypeStruct((b, nr, LANES), F32),
        compiler_params=_cparams(("parallel", "arbitrary")),
    )(page_table, *([slc_t] * pages), new_t, q32, msk32, bias_s, win_t, wnew_t, bias_w, oc32, gate32)


def _t5_bucket(rel):
    n = jnp.maximum(rel, 0)
    nf = jnp.maximum(n, T5_EXACT).astype(jnp.float32)
    large = T5_EXACT + (jnp.log(nf / T5_EXACT) / math.log(T5_MAX_DIST / T5_EXACT)
                        * (T5_BUCKETS - T5_EXACT)).astype(jnp.int32)
    return jnp.where(n < T5_EXACT, n, jnp.minimum(large, T5_BUCKETS - 1))


def _bias_1d(t5_table):
    tbl = t5_table.astype(F32)
    f1d = tbl[_t5_bucket(jnp.arange(BIAS_SPAN, dtype=jnp.int32))].T
    return f1d, tbl[T5_BUCKETS - 1][:, None]


def _hankel(v, n_rows, n_cols):
    length = v.shape[1]
    assert length >= n_rows + n_cols
    flat = jnp.tile(v, (1, n_rows + 1))[:, :n_rows * (length + 1)]
    return flat.reshape(v.shape[0], n_rows, length + 1)[:, :, :n_cols]


def _neg(*shape):
    return jnp.full(shape, NEG_INF, F32)


def _prompt_tables(t5_table, nsub):
    f1d, far = _bias_1d(t5_table)
    nh = f1d.shape[0]
    span = TQ - 1 + 8 * LANES + LANES
    sel1 = jnp.concatenate([_neg(nh, TQ - 1), (f1d - far)[:, :span - (TQ - 1)]], axis=1)
    win1 = jnp.concatenate([_neg(nh, TQ - 1), f1d[:, :WINDOW], _neg(nh, span - (TQ - 1) - WINDOW)], axis=1)

    def tiles(v):
        h = _hankel(v, TQ, 8 * LANES).reshape(nh, TQ, 8, LANES)
        return jnp.flip(h, axis=3).transpose(0, 2, 1, 3)

    bts = jnp.concatenate([_neg(nh, 1, TQ, LANES), tiles(sel1), jnp.zeros((nh, 1, TQ, LANES), F32)], axis=1)
    n_w = WINDOW // TQ + 1
    btw = jnp.concatenate([tiles(win1)[:, :n_w], _neg(nh, 1, TQ, LANES)], axis=1)
    lead = CMP_LEN - 1 + 8 * CMP_STRIDE
    g2 = jnp.concatenate([jnp.broadcast_to(f1d[:, :1], (nh, lead)), f1d[:, :BIAS_SPAN - lead]], axis=1)
    n_r = 81
    h2 = _hankel(g2, TQ, CMP_STRIDE * (n_r - 1) + 1)[:, :, ::CMP_STRIDE]
    mid = jnp.flip(h2.transpose(0, 2, 1), axis=1)
    n_top = nsub - (n_r - 9)
    if n_top < 0:
        mid, n_top = mid[:, -n_top:], 0
    tc = jnp.concatenate([jnp.broadcast_to(far[:, :, None], (nh, n_top, TQ)), mid,
                          jnp.zeros((nh, 2 * nsub - n_top - mid.shape[1], TQ), F32)], axis=1)
    return bts, btw, tc


def _sample_tables(t5_table, past_len, ts, w_buf, n_keys):
    f1d, far = _bias_1d(t5_table)
    nh = f1d.shape[0]
    sel_rows, win_rows = [], []
    for tok in range(ts):
        n_vis = past_len + tok + 1
        near = min(BIAS_SPAN, n_vis)
        sel_rows.append(jnp.concatenate([jnp.zeros((nh, n_vis - near), F32),
                                         jnp.flip((f1d - far)[:, :near], axis=1),
                                         _neg(nh, n_keys - n_vis)], axis=1))
        win_rows.append(jnp.concatenate([_neg(nh, tok + 1 + w_buf - WINDOW), jnp.flip(f1d[:, :WINDOW], axis=1),
                                         _neg(nh, LANES - tok - 1)], axis=1))
    stack = lambda rows: jnp.stack(rows, axis=0).reshape(ts * nh, -1)
    return stack(sel_rows), stack(win_rows)


def _importance_matrix(nb, nsub):
    a = np.zeros((nb, nsub), np.float32)
    for blk in range(nb):
        for j in range(4 * blk, 4 * blk + 4):
            for n in (j, j - 1):
                if 0 <= n < nsub:
                    a[blk, n] += 1.0
    return jnp.asarray(a, BF16)


def _sigma_cols():
    return np.concatenate([np.arange(64) + 64 * (4 * (s % 2) + s // 2) for s in range(8)])


def _prep_weights(g_pre, g_post, w_in, b_fgate, cmp_pos, w_cmp1, w_cmp2, w_out, w_pproj, w_pgate):
    sig = _sigma_cols()
    cols = np.concatenate([sig, np.arange(512, 1280), 1304 + sig, np.arange(1816, 3352), np.arange(3360, 3872),
                           np.arange(3352, 3360), np.arange(1280, 1304)])
    w_perm = jnp.concatenate([w_in[:, cols], jnp.zeros((D_MODEL, N_PROJ - cols.size), w_in.dtype)], 1).astype(BF16)
    bf_pad = jnp.zeros((1, LANES), F32).at[0, :H_FOX].set(b_fgate.astype(F32))
    wo = jnp.concatenate([w_out[sig], w_out[512:]], 0).astype(BF16)
    w1r = w_cmp1.reshape(2, 2, CMP_STRIDE, HEAD_DIM, CMP_HIDDEN)
    eye = jnp.eye(2, dtype=w_cmp1.dtype)
    wa = jnp.einsum('easdf,ex,gy->asegdxyf', w1r, eye, eye).reshape(2, 4096, 512).astype(BF16)
    pos = cmp_pos.reshape(2, 2, CMP_STRIDE, 1, HEAD_DIM)
    pos2 = jnp.broadcast_to(pos.transpose(1, 2, 0, 3, 4), (2, CMP_STRIDE, 2, KV_NSA, HEAD_DIM)).reshape(2, 4096)
    w2 = []
    for e in range(2):
        blk = jnp.zeros((256, LANES), w_cmp2.dtype)
        blk = blk.at[0:128, 0:64].set(w_cmp2[e]).at[128:256, 64:128].set(w_cmp2[e])
        w2.append(blk.astype(BF16))
    return dict(w_perm=w_perm, bf_pad=bf_pad, g_pre=g_pre.reshape(1, D_MODEL).astype(F32),
                g_post=g_post.reshape(1, D_MODEL).astype(F32), wo=wo, wpp=w_pproj.astype(BF16),
                wpg=w_pgate.astype(BF16), wa0=wa[0], wa1=wa[1], pos2=pos2.astype(F32), w2k=w2[0], w2v=w2[1])


def _compressed_branch(qn3, xflat, b, nsub, wts, tc, amat, i_off):
    pflat = _cmp1(xflat, wts['pos2'], wts['wa0'], wts['wa1'])
    ck, cvt = _cmp2(pflat.reshape(b, nsub, 1024), wts['w2k'], wts['w2v'])
    return _nsa_cmp(qn3, ck, cvt, tc, amat, i_off)


def _prompt_group(x3d, p3d, wts, t5_table):
    b, t, _ = x3d.shape
    rows = b * t
    (kvc, small, zn, zf, qn, qf, kvs_bf, kvw_bf, kvf_bf, kvc_t, kvs_t, kvw_t, kvf_t, logf_t) = _proj(
        x3d, wts['g_pre'], wts['w_perm'], wts['bf_pad'], True)
    r3 = lambda a: a.reshape(b, t, a.shape[-1])
    nsub = t // CMP_STRIDE
    nb = -(-(t // SEL_BLOCK) // LANES) * LANES
    bts, btw, tc = _prompt_tables(t5_table, nsub)
    cum, cumt = _cumsum(r3(small))
    o_f = _fox(r3(qf), r3(kvf_bf), cum, cumt)
    o_c, msk = _compressed_branch(r3(qn), kvc.reshape(rows // CMP_STRIDE, 4096), b, nsub, wts, tc,
                                  _importance_matrix(nb, nsub), 0)
    o_n = _nsa_sw(r3(qn), r3(kvs_bf), r3(kvw_bf), msk, bts, btw, o_c, r3(small))
    y = _mix_out(x3d.reshape(rows, D_MODEL), o_n.reshape(rows, 512), zn, o_f.reshape(rows, 512), zf,
                 p3d.reshape(rows, PLE_DIM), wts['wo'], wts['g_post'], wts['wpp'], wts['wpg'])
    def rows_out(a_t, n):
        return jnp.moveaxis(a_t.reshape(b, 2, n, HEAD_DIM, a_t.shape[-1]), -1, 1)[:, None]

    return (y.reshape(b, t, D_MODEL), rows_out(kvc_t, KV_NSA), rows_out(kvs_t, KV_NSA), rows_out(kvf_t, H_FOX),
            jnp.swapaxes(logf_t, 1, 2)[:, None], rows_out(kvw_t[:, :, t - min(WINDOW, t):], KV_NSA))


def _token_minor(a, ts):
    return jnp.pad(jnp.swapaxes(a[:, :ts], 1, 2), ((0, 0), (0, 0), (0, LANES - ts)))


def _sample_group(x3d, p3d, wts, t5_table, caches, page_table):
    b, ts, _ = x3d.shape
    n_pages = page_table.shape[1]
    past_len = n_pages * PAGE_SIZE
    n_keys = past_len + _pages_per_step(n_pages, NSA_PAGES)[0] * PAGE_SIZE
    pad_rows = lambda a, n: jnp.pad(a, ((0, 0), (0, n - a.shape[1]), (0, 0)))
    r3 = lambda a: a.reshape(b, TQ, a.shape[-1])
    kvc, small, zn, zf, qn, qf, kvs_bf, kvw_bf, kvf_bf, kvs, kvw, kvf = map(
        r3, _proj(pad_rows(x3d, TQ), wts['g_pre'], wts['w_perm'], wts['bf_pad'], False))

    ck_t = _logf_cumsum(caches['logf_t'], page_table, _token_minor(small[..., :H_FOX], ts))
    cq = jnp.swapaxes(ck_t[:, n_pages, :, :ts], 1, 2).reshape(b, ts * H_FOX, 1)
    eye = jnp.eye(H_FOX, dtype=BF16)
    qbd = (qf[:, :ts].reshape(b, ts, H_FOX, 1, HEAD_DIM) * eye[None, None, :, :, None]).reshape(b, ts * H_FOX, D_FOX)
    o_f = _fox_sample(caches['fox_t'], page_table, _token_minor(kvf_bf, ts), ck_t, qbd,
                      jnp.broadcast_to(cq, (b, ts * H_FOX, LANES)), ts)

    nsub = past_len // CMP_STRIDE
    nb = -(-(past_len // SEL_BLOCK + 2) // LANES) * LANES
    _, _, tc = _prompt_tables(t5_table, nsub)
    p3 = _cmp1_paged(caches['cmp_t'], page_table, wts['pos2'], wts['wa0'], wts['wa1'])
    ck, cvt = _cmp2(p3, wts['w2k'], wts['w2v'])
    o_c, msk = _nsa_cmp(qn, ck, cvt, tc, _importance_matrix(nb, nsub), past_len // TQ)

    def rows_gh(a):
        a = a.reshape(b, ts, GRP, KV_NSA, HEAD_DIM).transpose(0, 1, 3, 2, 4)
        z = jnp.zeros_like(a[:, :, 0])
        return jnp.stack([jnp.concatenate([a[:, :, 0], z], -1), jnp.concatenate([z, a[:, :, 1]], -1)],
                         axis=2).reshape(b, ts * 8, LANES)

    msk32 = jnp.broadcast_to(msk[:, 0, :, :ts].transpose(0, 2, 1, 3)[:, :, :, None, :],
                             (b, ts, KV_NSA, GRP, nb)).reshape(b, ts * 8, nb)
    gate32 = jnp.pad(small[:, :ts, 8:32].reshape(b, ts * 8, 3), ((0, 0), (0, 0), (0, LANES - 3)))
    bias_s, bias_w = _sample_tables(t5_table, past_len, ts, caches['win_t'].shape[-1], n_keys)
    on32 = _nsa_sample(caches['slc_t'], page_table, _token_minor(kvs_bf, ts), rows_gh(qn[:, :ts]), msk32, bias_s,
                       caches['win_t'], _token_minor(kvw_bf, ts), bias_w, rows_gh(o_c[:, :ts]), gate32)
    on5 = on32.reshape(b, ts, KV_NSA, GRP, LANES)
    o_n = jnp.stack([on5[:, :, 0, :, :HEAD_DIM], on5[:, :, 1, :, HEAD_DIM:]], axis=3).reshape(b, ts, D_NSA)

    n8 = SUBLANES
    y = _mix_out(pad_rows(x3d, n8).reshape(b * n8, D_MODEL), pad_rows(o_n, n8).reshape(b * n8, 512),
                 zn[:, :n8].reshape(b * n8, 512), o_f.reshape(b * n8, 512), zf[:, :n8].reshape(b * n8, 512),
                 pad_rows(p3d, n8).reshape(b * n8, PLE_DIM), wts['wo'], wts['g_post'], wts['wpp'], wts['wpg'])
    return (y.reshape(b, n8, D_MODEL)[:, :ts], kvc[:, :ts], kvs[:, :ts], kvf[:, :ts], small[:, :ts, :H_FOX],
            kvw[:, :ts])


def kernel(x_prompt, x_sample, cache_cmp_kv, cache_slc_kv, cache_fox_kv, cache_fox_logf, state_win_kv, page_table, p_prompt, p_sample, g_pre, g_post, w_in, b_fgate, cmp_pos, w_cmp1, w_cmp2, w_out, w_pproj, w_pgate, t5_table):
    assert g_pre.shape[0] == 1, "single-layer trunk"
    bp, t, _ = x_prompt.shape
    bs, ts, _ = x_sample.shape
    n_pool = cache_cmp_kv.shape[0]
    n_pages = page_table.shape[1]
    past_len = n_pages * PAGE_SIZE
    w_buf = state_win_kv.shape[2]
    assert t % CUM_ROWS == 0 and w_buf == WINDOW and ts <= SUBLANES

    wts = _prep_weights(g_pre[0], g_post[0], w_in[0], b_fgate[0], cmp_pos[0], w_cmp1[0], w_cmp2[0], w_out[0],
                        w_pproj[0], w_pgate[0])
    kv5 = lambda a, n: a.reshape(a.shape[0], 1, a.shape[1], 2, n, HEAD_DIM)

    outs_p = _prompt_group(x_prompt, p_prompt[0], wts, t5_table)

    feat_major = lambda a: jnp.moveaxis(a[:, 0], 1, -1).reshape(a.shape[0], -1, a.shape[2])
    caches = dict(cmp_t=feat_major(cache_cmp_kv), slc_t=feat_major(cache_slc_kv), fox_t=feat_major(cache_fox_kv),
                  logf_t=feat_major(cache_fox_logf), win_t=feat_major(state_win_kv))
    ys, kvc_s, kvs_s, kvf_s, logf_s, kvw_s = _sample_group(x_sample, p_sample[0], wts, t5_table, caches, page_table)
    win_new = jnp.concatenate([state_win_kv[:, 0].reshape(bs, w_buf, 256), kvw_s], axis=1)
    w_keep_s = min(WINDOW, past_len + ts)
    outs_s = (kv5(kvc_s, KV_NSA), kv5(kvs_s, KV_NSA), kv5(kvf_s, H_FOX), logf_s[:, None],
              kv5(win_new[:, win_new.shape[1] - w_keep_s:], KV_NSA))
    return (outs_p[0], ys) + outs_p[1:] + outs_s
```
